```python
import math
import jax
import jax.numpy as jnp
from jax import lax
import numpy as np


D_MODEL = 1024
BATCH = 2
SEQ = 8192
DEPTH = 2

GRID_W = 64
CTX_LEN = 256
N_BRANCH = 4
BRANCH_W = 512
CONV_W = 3
N_DIFF_HEADS = 4
DIFF_HEAD_DIM = 64
DIFF_V_DIM = 2 * DIFF_HEAD_DIM
QK_W = N_DIFF_HEADS * 2 * DIFF_HEAD_DIM
V_W = N_DIFF_HEADS * DIFF_V_DIM
Q_BLOCK = 128
ROPE_BASE = 10000.0
ROPE_PAIRS = DIFF_HEAD_DIM // 4
HYENA_ORDER = 2
HYENA_BANDS = 16
HYENA_EMB = 1 + 2 * HYENA_BANDS
HYENA_HIDDEN = 64
HYENA_SHIFT = 0.05
S5_GROUP = 16
S5_GROUPS = BRANCH_W // S5_GROUP
S5_STATE = 64
ALPHA = (2.0 * DEPTH) ** 0.25
BETA = (8.0 * DEPTH) ** -0.25
LN_EPS = 1e-5
PROJ_SIZES = (QK_W, V_W, BRANCH_W, BRANCH_W, BRANCH_W, BRANCH_W, BRANCH_W, BRANCH_W, BRANCH_W, BRANCH_W, BRANCH_W, QK_W, V_W, BRANCH_W, N_BRANCH * D_MODEL)
PROJ_W = sum(PROJ_SIZES)
CTX_COLS = QK_W + V_W + BRANCH_W

kernel_name = 'hybrid_gated_conv_hyena_diffattn_s5_dit'


def split_cols(p, sizes):
    offs = []
    acc = 0
    for s in sizes[:-1]:
        acc += s
        offs.append(acc)
    return jnp.split(p, offs, axis=-1)


def layer_norm(x, g, b):
    xf = x.astype(jnp.float32)
    mu = jnp.mean(xf, -1, keepdims=True)
    var = jnp.mean(jnp.square(xf - mu), -1, keepdims=True)
    return ((xf - mu) * lax.rsqrt(var + LN_EPS) * g + b).astype(x.dtype)


def short_conv(u, w):
    up = jnp.pad(u, ((0, 0), (1, 1), (0, 0)))
    return up[:, :-2] * w[0] + up[:, 1:-1] * w[1] + up[:, 2:] * w[2]


def hyena_filters_fft(L, w1, b1, w2, b2, w3, freq, deltas):
    f32 = jnp.float32
    t = jnp.linspace(0.0, 1.0, L, dtype=f32)[:, None]
    w = 2.0 * math.pi * jnp.arange(L, dtype=f32)[:, None] / L
    f = jnp.linspace(1e-4, HYENA_BANDS - 1, HYENA_BANDS, dtype=f32)[None, :]
    z = jnp.concatenate([t, jnp.cos(f * w), -jnp.sin(f * w)], axis=-1)
    h = jnp.sin(freq[0].astype(f32) * (z @ w1.astype(f32) + b1.astype(f32)))
    h = jnp.sin(freq[1].astype(f32) * (h @ w2.astype(f32) + b2.astype(f32)))
    h = (h @ w3.astype(f32)).reshape(L, HYENA_ORDER, 2, BRANCH_W)
    decay = jnp.exp(-t[:, :, None, None] * jnp.abs(deltas.astype(f32)))
    h = h * (decay + HYENA_SHIFT)
    hf, hb = h[:, :, 0], h[:, :, 1]
    taps = jnp.concatenate([hf, jnp.zeros_like(hf[:1]), hb[:0:-1]], axis=0)
    taps = taps / (jnp.sum(jnp.abs(taps), axis=0, keepdims=True) + 1e-6)
    return jnp.fft.rfft(taps, axis=0)


def long_conv(z, tfft, bias):
    L = z.shape[1]
    zf = z.astype(jnp.float32)
    y = jnp.fft.irfft(jnp.fft.rfft(zf, n=2 * L, axis=1) * tfft, n=2 * L, axis=1)[:, :L]
    return (y + bias.astype(jnp.float32) * zf).astype(z.dtype)


def axial_rope_tables(n_lat):
    rows = n_lat // GRID_W
    row = jnp.broadcast_to(jnp.arange(rows)[:, None], (rows, GRID_W)).reshape(-1)
    col = jnp.broadcast_to(jnp.arange(GRID_W)[None, :], (rows, GRID_W)).reshape(-1)
    half = DIFF_HEAD_DIM // 2
    inv = 1.0 / (ROPE_BASE ** (jnp.arange(0, half, 2, dtype=jnp.float32) / half))
    ang = jnp.concatenate([row[:, None] * inv, col[:, None] * inv], axis=-1)
    return jnp.cos(ang), jnp.sin(ang)


def apply_axial_rope(t, cos, sin):
    shp = t.shape
    tf = t.astype(jnp.float32).reshape(shp[:-1] + (2, 2, ROPE_PAIRS))
    t1, t2 = tf[..., 0, :], tf[..., 1, :]
    cs = cos.reshape(-1, 1, 1, 2, ROPE_PAIRS)
    sn = sin.reshape(-1, 1, 1, 2, ROPE_PAIRS)
    out = jnp.stack([t1 * cs - t2 * sn, t1 * sn + t2 * cs], axis=-2)
    return out.reshape(shp).astype(t.dtype)


def diff_attn(q, k, v, lam):
    s = jnp.einsum('bqhcd,bkhcd->bhcqk', q, k, preferred_element_type=jnp.float32) * (DIFF_HEAD_DIM ** -0.5)
    p = jax.nn.softmax(s, axis=-1)
    w = p[:, :, 0] - lam * p[:, :, 1]
    return jnp.einsum('bhqk,bkhe->bqhe', w.astype(v.dtype), v)


def latent_diff_attention(q, k, v, kc, vc, lam):
    b, n, h, _, d = q.shape
    k_all = jnp.concatenate([k, kc], axis=1)
    v_all = jnp.concatenate([v, vc], axis=1)
    qb = q.reshape(b, n // Q_BLOCK, Q_BLOCK, h, 2, d).swapaxes(0, 1)
    ob = lax.map(lambda qi: diff_attn(qi, k_all, v_all, lam), qb)
    return ob.swapaxes(0, 1).reshape(b, n, h, v.shape[-1])


def diff_head_norm(o, g, lam_init):
    of = o.astype(jnp.float32)
    of = of * lax.rsqrt(jnp.mean(of * of, -1, keepdims=True) + 1e-5) * g * (1.0 - lam_init)
    return of.reshape(o.shape[:2] + (V_W,)).astype(o.dtype)


def s5_discretize(a_re, a_im, log_step, b_re, b_im):
    f32 = jnp.float32
    lam = lax.complex(a_re.astype(f32), a_im.astype(f32))
    dt = jnp.exp(log_step.astype(f32))[:, None]
    a_bar = jnp.exp(lam * dt)
    b_bar = ((a_bar - 1.0) / lam)[..., None] * lax.complex(b_re.astype(f32), b_im.astype(f32))
    return a_bar, b_bar


def _lin_rec(e1, e2):
    a1, b1 = e1
    a2, b2 = e2
    return a2 * a1, a2 * b1 + b2


def s5_scan(u, a_bar, b_bar, s0, reverse):
    bu = jnp.einsum('blgc,gpc->blgp', u.astype(jnp.float32).astype(jnp.complex64), b_bar)
    if s0 is not None:
        edge = -1 if reverse else 0
        bu = bu.at[:, edge].add(a_bar * s0)
    a = jnp.broadcast_to(a_bar, bu.shape)
    _, s = lax.associative_scan(_lin_rec, (a, bu), axis=1, reverse=reverse)
    return s


def mixer_sublayer(parts, y_att, s_f, s_b, conv_a, conv_h, hyena_p, hy_bias, c_f, c_b, s5_d, s5_w_glu, w_branch, w_out):
    (_, _, u, a_b, a_c, a_x, a_g, h_v, h_x1, h_x2, h_g, _, att_g, s5_g, merge) = parts
    bsz, L, _ = u.shape
    y_a = a_b * short_conv(a_c * a_x, conv_a) * jax.nn.silu(a_g)
    tfft = hyena_filters_fft(L, *hyena_p)
    hu = short_conv(jnp.concatenate([h_v, h_x1, h_x2], axis=-1), conv_h)
    v0, x1, x2 = jnp.split(hu, 3, axis=-1)
    z = x1 * long_conv(v0, tfft[:, 0], hy_bias[0])
    z = x2 * long_conv(z, tfft[:, 1], hy_bias[1])
    y_h = z * jax.nn.silu(h_g)
    y_c = y_att * jax.nn.silu(att_g)
    y = jnp.real(jnp.einsum('blgp,gcp->blgc', s_f, c_f) + jnp.einsum('blgp,gcp->blgc', s_b, c_b))
    y = y.reshape(bsz, L, BRANCH_W).astype(u.dtype) + s5_d * u
    zg = jax.nn.gelu(y)
    y_d = zg * jax.nn.sigmoid(zg @ s5_w_glu) * jax.nn.silu(s5_g)
    ys = jnp.stack([y_a, y_h, y_c, y_d], axis=-2)
    proj = jnp.einsum('blnw,nwd->blnd', ys, w_branch)
    gates = jax.nn.sigmoid(merge.reshape(proj.shape))
    return jnp.sum(gates * proj, axis=-2) @ w_out


def setup_inputs(seed: int = 0) -> dict:
    key = jax.random.key(seed)
    ks = iter(jax.random.split(key, 48))
    f32 = jnp.float32

    def nrm(shape, s):
        return s * jax.random.normal(next(ks), shape, f32)

    D = D_MODEL
    W = BRANCH_W
    G = S5_GROUPS
    P = S5_STATE
    x = nrm((BATCH, SEQ, D), 1.0)
    c = nrm((BATCH, D), 1.0)
    ctx = nrm((BATCH, CTX_LEN, D), 1.0)
    c_ctx = nrm((D,), 1.0)
    w_mod = nrm((DEPTH, D, 3 * D), D ** -0.5)
    b_mod = nrm((DEPTH, 3 * D), 0.02)
    w_in = nrm((DEPTH, D, PROJ_W), D ** -0.5)
    conv_a = nrm((DEPTH, CONV_W, W), CONV_W ** -0.5)
    conv_h = nrm((DEPTH, CONV_W, 3 * W), CONV_W ** -0.5)
    hy_w1 = nrm((DEPTH, HYENA_EMB, HYENA_HIDDEN), HYENA_EMB ** -0.5)
    hy_b1 = nrm((DEPTH, HYENA_HIDDEN), 0.02)
    hy_w2 = nrm((DEPTH, HYENA_HIDDEN, HYENA_HIDDEN), HYENA_HIDDEN ** -0.5)
    hy_b2 = nrm((DEPTH, HYENA_HIDDEN), 0.02)
    hy_w3 = nrm((DEPTH, HYENA_HIDDEN, HYENA_ORDER * 2 * W), HYENA_HIDDEN ** -0.5)
    hy_freq = 1.0 + nrm((DEPTH, 2, HYENA_HIDDEN), 0.02)
    target = 1e-2
    base = jnp.abs(jnp.linspace(math.log(target) / 1.5, math.log(target) / 0.3, W, dtype=f32))
    hy_delta = jnp.broadcast_to(base, (DEPTH, HYENA_ORDER, 2, W)) * jnp.exp(nrm((DEPTH, HYENA_ORDER, 2, W), 0.01))
    hy_bias = nrm((DEPTH, HYENA_ORDER, W), 1.0)
    lam_q1 = nrm((DEPTH, DIFF_HEAD_DIM), 0.1)
    lam_k1 = nrm((DEPTH, DIFF_HEAD_DIM), 0.1)
    lam_q2 = nrm((DEPTH, DIFF_HEAD_DIM), 0.1)
    lam_k2 = nrm((DEPTH, DIFF_HEAD_DIM), 0.1)
    attn_norm_g = 1.0 + nrm((DEPTH, DIFF_V_DIM), 0.02)
    n_idx = jnp.arange(P, dtype=f32)
    s5_a_re = -0.5 + nrm((DEPTH, 2, G, P), 0.01)
    s5_a_im = math.pi * n_idx + nrm((DEPTH, 2, G, P), 0.01)
    s5_log_step = jax.random.uniform(next(ks), (DEPTH, 2, G), f32, math.log(1e-3), math.log(1e-1))
    s5_b_re = nrm((DEPTH, 2, G, P, S5_GROUP), (2 * S5_GROUP) ** -0.5)
    s5_b_im = nrm((DEPTH, 2, G, P, S5_GROUP), (2 * S5_GROUP) ** -0.5)
    s5_c_re = nrm((DEPTH, 2, G, S5_GROUP, P), (2 * P) ** -0.5)
    s5_c_im = nrm((DEPTH, 2, G, S5_GROUP, P), (2 * P) ** -0.5)
    s5_d = nrm((DEPTH, W), 1.0)
    s5_w_glu = nrm((DEPTH, W, W), W ** -0.5)
    w_branch = nrm((DEPTH, N_BRANCH, W, D), (W ** -0.5) * BETA)
    w_out = nrm((DEPTH, D, D), (D ** -0.5) * BETA)
    ln_g = 1.0 + nrm((DEPTH, D), 0.02)
    ln_b = nrm((DEPTH, D), 0.02)
    return {'x': x, 'c': c, 'ctx': ctx, 'c_ctx': c_ctx, 'w_mod': w_mod, 'b_mod': b_mod, 'w_in': w_in,
            'conv_a': conv_a, 'conv_h': conv_h, 'hy_w1': hy_w1, 'hy_b1': hy_b1, 'hy_w2': hy_w2, 'hy_b2': hy_b2,
            'hy_w3': hy_w3, 'hy_freq': hy_freq, 'hy_delta': hy_delta, 'hy_bias': hy_bias,
            'lam_q1': lam_q1, 'lam_k1': lam_k1, 'lam_q2': lam_q2, 'lam_k2': lam_k2, 'attn_norm_g': attn_norm_g,
            's5_a_re': s5_a_re, 's5_a_im': s5_a_im, 's5_log_step': s5_log_step, 's5_b_re': s5_b_re,
            's5_b_im': s5_b_im, 's5_c_re': s5_c_re, 's5_c_im': s5_c_im, 's5_d': s5_d, 's5_w_glu': s5_w_glu,
            'w_branch': w_branch, 'w_out': w_out, 'ln_g': ln_g, 'ln_b': ln_b}


def reference(x, c, ctx, c_ctx, w_mod, b_mod, w_in, conv_a, conv_h, hy_w1, hy_b1, hy_w2, hy_b2, hy_w3, hy_freq, hy_delta, hy_bias, lam_q1, lam_k1, lam_q2, lam_k2, attn_norm_g, s5_a_re, s5_a_im, s5_log_step, s5_b_re, s5_b_im, s5_c_re, s5_c_im, s5_d, s5_w_glu, w_branch, w_out, ln_g, ln_b):
    bsz, n_lat, _ = x.shape
    n_ctx = ctx.shape[1]
    H, d, E = N_DIFF_HEADS, DIFF_HEAD_DIM, DIFF_V_DIM
    G = S5_GROUPS
    cos, sin = axial_rope_tables(n_lat)
    xc = ctx
    for l in range(DEPTH):
        last = l == DEPTH - 1
        shift, scale, gate = jnp.split(jax.nn.silu(c) @ w_mod[l] + b_mod[l], 3, axis=-1)
        shift_c, scale_c, gate_c = jnp.split(jax.nn.silu(c_ctx) @ w_mod[l] + b_mod[l], 3, axis=-1)
        h = x * (1.0 + scale[:, None]) + shift[:, None]
        hc = xc * (1.0 + scale_c) + shift_c
        lam_init = 0.8 - 0.6 * math.exp(-0.3 * l)
        lam = (jnp.exp(jnp.sum(lam_q1[l] * lam_k1[l]).astype(jnp.float32))
               - jnp.exp(jnp.sum(lam_q2[l] * lam_k2[l]).astype(jnp.float32)) + lam_init)
        a_fw, bb_fw = s5_discretize(s5_a_re[l, 0], s5_a_im[l, 0], s5_log_step[l, 0], s5_b_re[l, 0], s5_b_im[l, 0])
        a_bw, bb_bw = s5_discretize(s5_a_re[l, 1], s5_a_im[l, 1], s5_log_step[l, 1], s5_b_re[l, 1], s5_b_im[l, 1])
        c_fw = lax.complex(s5_c_re[l, 0].astype(jnp.float32), s5_c_im[l, 0].astype(jnp.float32))
        c_bw = lax.complex(s5_c_re[l, 1].astype(jnp.float32), s5_c_im[l, 1].astype(jnp.float32))
        hyena_p = (hy_w1[l], hy_b1[l], hy_w2[l], hy_b2[l], hy_w3[l], hy_freq[l], hy_delta[l])
        w_in_l = w_in[l]
        if last:
            kc, vc, uc = split_cols(hc @ w_in_l[:, :CTX_COLS], PROJ_SIZES[:3])
            parts_c = None
        else:
            parts_c = split_cols(hc @ w_in_l, PROJ_SIZES)
            kc, vc, uc = parts_c[0], parts_c[1], parts_c[2]
        kc = kc.reshape(bsz, n_ctx, H, 2, d)
        vc = vc.reshape(bsz, n_ctx, H, E)
        uc_g = uc.reshape(bsz, n_ctx, G, S5_GROUP)
        sc_f = s5_scan(uc_g, a_fw, bb_fw, None, False)
        sc_b = s5_scan(uc_g, a_bw, bb_bw, None, True)
        parts = split_cols(h @ w_in_l, PROJ_SIZES)
        q = apply_axial_rope(parts[11].reshape(bsz, n_lat, H, 2, d), cos, sin)
        k = apply_axial_rope(parts[0].reshape(bsz, n_lat, H, 2, d), cos, sin)
        v = parts[1].reshape(bsz, n_lat, H, E)
        y_att = diff_head_norm(latent_diff_attention(q, k, v, kc, vc, lam), attn_norm_g[l], lam_init)
        u_g = parts[2].reshape(bsz, n_lat, G, S5_GROUP)
        s_f = s5_scan(u_g, a_fw, bb_fw, sc_f[:, -1], False)
        s_b = s5_scan(u_g, a_bw, bb_bw, sc_b[:, 0], True)
        y = mixer_sublayer(parts, y_att, s_f, s_b, conv_a[l], conv_h[l], hyena_p, hy_bias[l], c_fw, c_bw,
                           s5_d[l], s5_w_glu[l], w_branch[l], w_out[l])
        x_new = layer_norm(ALPHA * x + gate[:, None] * y, ln_g[l], ln_b[l])
        if not last:
            qc = parts_c[11].reshape(bsz, n_ctx, H, 2, d)
            yc_att = diff_head_norm(diff_attn(qc, kc, vc, lam), attn_norm_g[l], lam_init)
            yc = mixer_sublayer(parts_c, yc_att, sc_f, sc_b, conv_a[l], conv_h[l], hyena_p, hy_bias[l], c_fw, c_bw,
                                s5_d[l], s5_w_glu[l], w_branch[l], w_out[l])
            xc = layer_norm(ALPHA * xc + gate_c * yc, ln_g[l], ln_b[l])
        x = x_new
    return x
```

```python
import functools
import math

import jax
import jax.numpy as jnp
from jax import lax
from jax.experimental import pallas as pl
from jax.experimental.pallas import tpu as pltpu

F32 = jnp.float32
BF16 = jnp.bfloat16
HIGHEST = lax.Precision.HIGHEST

D_MODEL = 1024
BRANCH_W = 512
N_HEADS = 4
HEAD_DIM = 64
V_DIM = 128
GRID_W = 64
ROPE_BASE = 10000.0
HYENA_BANDS = 16
HYENA_HIDDEN = 64
HYENA_SHIFT = 0.05
S5_GROUP = 16
S5_GROUPS = 32
S5_STATE = 64
S5_SUPER = 4
S5_SW = S5_GROUPS * S5_STATE
LN_EPS = 1e-5
N_BRANCH = 4
UNIT = 512
N_UNITS = 22
PROJ_W = N_UNITS * UNIT
UNIT_PERM = (0, 1, 11, 12, 2, 4, 5, 7, 8, 9, 3, 6, 13, 10, 14, 15, 16, 17, 18, 19, 20, 21)
U_K, U_V, U_Q, U_ATTG, U_U, U_CONV, U_AB, U_S5G, U_HG, U_MERGE = 0, 1, 2, 3, 4, 5, 10, 12, 13, 14
DFT_N2 = 128
VMEM_LIMIT = 56 * 1024 * 1024


def _cparams(sem):
    return pltpu.CompilerParams(dimension_semantics=sem, vmem_limit_bytes=VMEM_LIMIT)


def _silu(v):
    return v * jax.nn.sigmoid(v)


def _mod_kernel(s_ref, w_ref, b_ref, o_ref):
    s = _silu(s_ref[...])
    o_ref[...] = jnp.dot(s, w_ref[...], preferred_element_type=F32, precision=HIGHEST) + b_ref[...]


def _modulation(cvec, w, b):
    n = w.shape[1]
    tn = 512
    return pl.pallas_call(
        _mod_kernel,
        out_shape=jax.ShapeDtypeStruct((8, n), F32),
        grid=(n // tn,),
        in_specs=[pl.BlockSpec((8, D_MODEL), lambda j: (0, 0)),
                  pl.BlockSpec((D_MODEL, tn), lambda j: (0, j)),
                  pl.BlockSpec((1, tn), lambda j: (0, j))],
        out_specs=pl.BlockSpec((8, tn), lambda j: (0, j)),
        compiler_params=_cparams(("parallel",)),
        name="modulation",
    )(cvec, w, b.reshape(1, n))


def _inproj_kernel(x_ref, mod_ref, w_ref, cos_ref, sin_ref, o_ref, h_ref, *, rope):
    j = pl.program_id(1)

    @pl.when(j == 0)
    def _():
        shift = mod_ref[0:1, :]
        scale = mod_ref[1:2, :]
        h_ref[...] = (x_ref[...] * (1.0 + scale) + shift).astype(BF16)

    acc = jnp.dot(h_ref[...], w_ref[...], preferred_element_type=F32)
    if not rope:
        o_ref[...] = acc.astype(o_ref.dtype)
        return

    @pl.when(j < 2)
    def _():
        cs = cos_ref[...]
        sn = sin_ref[...]
        lane = lax.broadcasted_iota(jnp.int32, cs.shape, 1)
        first = (lane % 32) < 16
        for cb in range(UNIT // 128):
            t = acc[:, cb * 128:(cb + 1) * 128]
            partner = jnp.where(first, pltpu.roll(t, 128 - 16, 1), pltpu.roll(t, 16, 1))
            o_ref[:, cb * 128:(cb + 1) * 128] = (t * cs + partner * sn).astype(o_ref.dtype)
        o_ref[:, UNIT:] = acc[:, UNIT:].astype(o_ref.dtype)

    @pl.when(j >= 2)
    def _():
        o_ref[...] = acc.astype(o_ref.dtype)


def _inproj(x2d, mod3, w, cos_t, sin_t, *, seq, rope, tm, ncols):
    m = x2d.shape[0]
    tn = 1024
    nt_seq = max(seq // tm, 1)
    nb = mod3.shape[0]
    mod_map = (lambda i, j: (i // nt_seq, 0, 0)) if nb > 1 else (lambda i, j: (0, 0, 0))
    return pl.pallas_call(
        functools.partial(_inproj_kernel, rope=rope),
        out_shape=jax.ShapeDtypeStruct((m, ncols), BF16),
        grid=(m // tm, ncols // tn),
        in_specs=[pl.BlockSpec((tm, D_MODEL), lambda i, j: (i, 0)),
                  pl.BlockSpec((None, 3, D_MODEL), mod_map),
                  pl.BlockSpec((D_MODEL, tn), lambda i, j: (0, j)),
                  pl.BlockSpec((tm, 128), lambda i, j: (i % nt_seq, 0)),
                  pl.BlockSpec((tm, 128), lambda i, j: (i % nt_seq, 0))],
        out_specs=pl.BlockSpec((tm, tn), lambda i, j: (i, j)),
        scratch_shapes=[pltpu.VMEM((tm, D_MODEL), BF16)],
        compiler_params=_cparams(("parallel", "arbitrary")),
        name="inproj_rope" if rope else "inproj_ctx",
    )(x2d, mod3, w, cos_t, sin_t)


def _attn_kernel(*refs, n_lat, tq, lam_init):
    if n_lat:
        q_ref, k_ref, v_ref, kc_ref, vc_ref, lamp_ref, g_ref, o_ref, qz, m_s, l_s, acc = refs
    else:
        q_ref, kc_ref, vc_ref, lamp_ref, g_ref, o_ref, qz, m_s, l_s, acc = refs
    ki = pl.program_id(3)

    @pl.when(ki == 0)
    def _():
        q = q_ref[...].astype(F32) * (HEAD_DIM ** -0.5)
        lane = lax.broadcasted_iota(jnp.int32, q.shape, 1)
        qz[0:tq, :] = jnp.where(lane < HEAD_DIM, q, 0.0).astype(BF16)
        qz[tq:, :] = jnp.where(lane >= HEAD_DIM, q, 0.0).astype(BF16)
        m_s[...] = jnp.full(m_s.shape, -jnp.inf, F32)
        l_s[...] = jnp.zeros(l_s.shape, F32)
        acc[...] = jnp.zeros(acc.shape, F32)

    def update(k, v):
        s = lax.dot_general(qz[...], k, (((1,), (1,)), ((), ())), preferred_element_type=F32)
        m_prev = m_s[...]
        m_new = jnp.maximum(m_prev, jnp.max(s, axis=1, keepdims=True))
        alpha = jnp.exp(m_prev - m_new)
        p = jnp.exp(s - m_new)
        l_s[...] = alpha * l_s[...] + jnp.sum(p, axis=1, keepdims=True)
        acc[...] = alpha * acc[...] + jnp.dot(p.astype(BF16), v, preferred_element_type=F32)
        m_s[...] = m_new

    if n_lat:
        @pl.when(ki < n_lat)
        def _():
            update(k_ref[...], v_ref[...])

        @pl.when(ki == n_lat)
        def _():
            update(kc_ref[...], vc_ref[...])
    else:
        update(kc_ref[...], vc_ref[...])

    @pl.when(ki == n_lat)
    def _():
        o = acc[...] / l_s[...]
        lp = lamp_ref[...]
        lam = (jnp.exp(jnp.sum(lp[0:1] * lp[1:2], axis=1, keepdims=True))
               - jnp.exp(jnp.sum(lp[2:3] * lp[3:4], axis=1, keepdims=True)) + lam_init)
        od = o[0:tq] - lam * o[tq:]
        od = od * lax.rsqrt(jnp.mean(od * od, axis=1, keepdims=True) + 1e-5) * g_ref[...] * (1.0 - lam_init)
        o_ref[...] = od.astype(o_ref.dtype)


def _attention(proj_q, proj_kv, proj_c, lamp, g, *, bsz, seq_q, seq_kv, seq_c, lam_init, tq, tk):
    nq = seq_q // tq
    n_lat = 0 if proj_kv is None else seq_kv // tk
    hq = U_Q * UNIT // 128
    hk = U_K * UNIT // 128
    hv = U_V * UNIT // 128
    in_specs = [pl.BlockSpec((tq, 128), lambda b, h, qi, ki: (b * nq + qi, hq + h))]
    args = [proj_q]
    if n_lat:
        kmap = lambda b, h, qi, ki: (b * n_lat + jnp.minimum(ki, n_lat - 1), hk + h)
        vmap_ = lambda b, h, qi, ki: (b * n_lat + jnp.minimum(ki, n_lat - 1), hv + h)
        in_specs += [pl.BlockSpec((tk, 128), kmap), pl.BlockSpec((tk, 128), vmap_)]
        args += [proj_kv, proj_kv]
    in_specs += [pl.BlockSpec((seq_c, 128), lambda b, h, qi, ki: (b, hk + h)),
                 pl.BlockSpec((seq_c, 128), lambda b, h, qi, ki: (b, hv + h)),
                 pl.BlockSpec((4, HEAD_DIM), lambda b, h, qi, ki: (0, 0)),
                 pl.BlockSpec((1, V_DIM), lambda b, h, qi, ki: (0, 0))]
    args += [proj_c, proj_c, lamp, g]
    return pl.pallas_call(
        functools.partial(_attn_kernel, n_lat=n_lat, tq=tq, lam_init=lam_init),
        out_shape=jax.ShapeDtypeStruct((bsz * seq_q, N_HEADS * V_DIM), BF16),
        grid=(bsz, N_HEADS, nq, n_lat + 1),
        in_specs=in_specs,
        out_specs=pl.BlockSpec((tq, V_DIM), lambda b, h, qi, ki: (b * nq + qi, h)),
        scratch_shapes=[pltpu.VMEM((2 * tq, 128), BF16), pltpu.VMEM((2 * tq, 1), F32),
                        pltpu.VMEM((2 * tq, 1), F32), pltpu.VMEM((2 * tq, V_DIM), F32)],
        compiler_params=_cparams(("parallel", "parallel", "parallel", "arbitrary")),
        name="diff_attn" if n_lat else "diff_attn_ctx",
    )(*args)


def _conv3(cur, prv, nxt, w):
    tm = cur.shape[0]
    row = lax.broadcasted_iota(jnp.int32, cur.shape, 0)
    dn = jnp.where(row == 0, prv, pltpu.roll(cur, 1, 0))
    up = jnp.where(row == tm - 1, nxt, pltpu.roll(cur, tm - 1, 0))
    return dn * w[0:1] + cur * w[1:2] + up * w[2:3]


def _convgate_kernel(main_ref, prev_ref, next_ref, ab_ref, ca_ref, ch_ref,
                     ya_ref, v0_ref, x1_ref, x2_ref, *, nt):
    i = pl.program_id(1)
    w_ = BRANCH_W
    m = main_ref[...].astype(F32)
    pv = jnp.where(i > 0, prev_ref[15:16, :].astype(F32), 0.0)
    nx = jnp.where(i < nt - 1, next_ref[0:1, :].astype(F32), 0.0)
    ab = ab_ref[...].astype(F32)
    p = m[:, 0:w_] * m[:, w_:2 * w_]
    p_prev = pv[:, 0:w_] * pv[:, w_:2 * w_]
    p_next = nx[:, 0:w_] * nx[:, w_:2 * w_]
    ya = ab[:, 0:w_] * _conv3(p, p_prev, p_next, ca_ref[...]) * _silu(ab[:, w_:])
    ya_ref[...] = ya.astype(ya_ref.dtype)
    ch = ch_ref[...]
    for n, ref in enumerate((v0_ref, x1_ref, x2_ref)):
        lo, hi = (2 + n) * w_, (3 + n) * w_
        ref[...] = _conv3(m[:, lo:hi], pv[:, lo:hi], nx[:, lo:hi], ch[:, n * w_:(n + 1) * w_]).astype(ref.dtype)


def _convgate(proj, conv_a, conv_h, *, bsz, seq, tm):
    m = bsz * seq
    nt = seq // tm
    cw = 5 * UNIT
    hb = 16
    nhb = m // hb
    out = jax.ShapeDtypeStruct((m, BRANCH_W), BF16)
    ospec = pl.BlockSpec((tm, BRANCH_W), lambda b, i: (b * nt + i, 0))
    return pl.pallas_call(
        functools.partial(_convgate_kernel, nt=nt),
        out_shape=(out, out, out, out),
        grid=(bsz, nt),
        in_specs=[pl.BlockSpec((tm, cw), lambda b, i: (b * nt + i, U_CONV * UNIT // cw)),
                  pl.BlockSpec((hb, cw), lambda b, i: (jnp.maximum((b * nt + i) * (tm // hb) - 1, 0), 1)),
                  pl.BlockSpec((hb, cw), lambda b, i: (jnp.minimum((b * nt + i + 1) * (tm // hb), nhb - 1), 1)),
                  pl.BlockSpec((tm, 2 * UNIT), lambda b, i: (b * nt + i, U_AB * UNIT // (2 * UNIT))),
                  pl.BlockSpec((3, BRANCH_W), lambda b, i: (0, 0)),
                  pl.BlockSpec((3, 3 * BRANCH_W), lambda b, i: (0, 0))],
        out_specs=(ospec, ospec, ospec, ospec),
        compiler_params=_cparams(("parallel", "parallel")),
        name="convgate",
    )(proj, proj, proj, proj, conv_a, conv_h)


def _taps_kernel(w1t_ref, w1c_ref, w1s_ref, b1_ref, w2_ref, b2_ref, w3_ref, fr_ref, dl_ref,
                 taps_ref, asum_ref, *, seq, tr):
    i = pl.program_id(0)
    mrow = i * tr + lax.broadcasted_iota(jnp.int32, (tr, 1), 0)
    src = jnp.where(mrow < seq, mrow, 2 * seq - mrow).astype(F32)
    t = src / (seq - 1.0)
    w = (2.0 * math.pi / seq) * src
    band = lax.broadcasted_iota(jnp.int32, (1, HYENA_BANDS), 1).astype(F32)
    f = 1e-4 + band * ((HYENA_BANDS - 1.0 - 1e-4) / (HYENA_BANDS - 1.0))
    ang = w * f
    pre = (t * w1t_ref[...]
           + jnp.dot(jnp.cos(ang), w1c_ref[...], preferred_element_type=F32, precision=HIGHEST)
           + jnp.dot(-jnp.sin(ang), w1s_ref[...], preferred_element_type=F32, precision=HIGHEST)
           + b1_ref[...])
    fr = fr_ref[...]
    h = jnp.sin(fr[0:1] * pre)
    h = jnp.sin(fr[1:2] * (jnp.dot(h, w2_ref[...], preferred_element_type=F32, precision=HIGHEST) + b2_ref[...]))
    h = jnp.dot(h, w3_ref[...], preferred_element_type=F32, precision=HIGHEST)
    decay = jnp.exp(-t * jnp.abs(dl_ref[...]))
    out = h * (decay + HYENA_SHIFT)
    out = jnp.where(mrow == seq, 0.0, out)
    taps_ref[...] = out

    @pl.when(i == 0)
    def _():
        asum_ref[...] = jnp.zeros(asum_ref.shape, F32)

    asum_ref[...] += jnp.sum(jnp.abs(out), axis=0, keepdims=True)


def _hyena_taps(hp, *, seq, tr):
    w1, b1, w2, b2, w3, freq, deltas = hp
    hh = HYENA_HIDDEN
    cw = 2 * BRANCH_W
    w3d = w3.reshape(hh, 2, 2, BRANCH_W).transpose(2, 0, 1, 3).reshape(2, hh, cw)
    dld = deltas.transpose(1, 0, 2).reshape(2, 1, cw)
    nt = 2 * seq // tr
    full = lambda shape: pl.BlockSpec(shape, lambda i: tuple(0 for _ in shape))
    return pl.pallas_call(
        functools.partial(_taps_kernel, seq=seq, tr=tr),
        out_shape=(jax.ShapeDtypeStruct((2 * seq, cw), F32), jax.ShapeDtypeStruct((1, cw), F32)),
        grid=(nt,),
        in_specs=[full((1, hh)), full((HYENA_BANDS, hh)), full((HYENA_BANDS, hh)), full((1, hh)),
                  full((hh, hh)), full((1, hh)),
                  pl.BlockSpec((None, hh, cw), lambda i: (i // (nt // 2), 0, 0)),
                  full((2, hh)),
                  pl.BlockSpec((None, 1, cw), lambda i: (i // (nt // 2), 0, 0))],
        out_specs=(pl.BlockSpec((tr, cw), lambda i: (i, 0)), pl.BlockSpec((1, cw), lambda i: (0, 0))),
        compiler_params=_cparams(("arbitrary",)),
        name="hyena_taps",
    )(w1[0:1], w1[1:1 + HYENA_BANDS], w1[1 + HYENA_BANDS:], b1.reshape(1, hh), w2, b2.reshape(1, hh),
      w3d, freq, dld)


def _lmm_kernel(a_ref, x_ref, *rest, gate):
    if gate:
        xg_ref, z_ref, bias_ref, o_ref = rest
    else:
        (o_ref,) = rest
    acc = jnp.dot(a_ref[...], x_ref[...].astype(BF16), preferred_element_type=F32)
    if gate:
        acc = xg_ref[...].astype(F32) * (acc + bias_ref[...] * z_ref[...].astype(F32))
    o_ref[...] = acc.astype(o_ref.dtype)


def _lmm(a, x, *, tn, out_dtype, gate_args=None, name):
    nb, k, n = x.shape
    ma = a.shape[0]
    in_specs = [pl.BlockSpec((ma, k), lambda b, j: (0, 0)), pl.BlockSpec((None, k, tn), lambda b, j: (b, 0, j))]
    args = [a, x]
    if gate_args is not None:
        xg, z, bias_t = gate_args
        in_specs += [pl.BlockSpec((None, ma, tn), lambda b, j: (b, 0, j)),
                     pl.BlockSpec((None, ma, tn), lambda b, j: (b, 0, j)),
                     pl.BlockSpec((1, tn), lambda b, j: (0, 0))]
        args += [xg, z, bias_t]
    return pl.pallas_call(
        functools.partial(_lmm_kernel, gate=gate_args is not None),
        out_shape=jax.ShapeDtypeStruct((nb, ma, n), out_dtype),
        grid=(nb, n // tn),
        in_specs=in_specs,
        out_specs=pl.BlockSpec((None, ma, tn), lambda b, j: (b, 0, j)),
        compiler_params=_cparams(("parallel", "parallel")),
        name=name,
    )(*args)


def _mid_kernel(a_ref, t_ref, *rest, conv):
    if conv:
        g_ref, u_ref, o_ref = rest
    else:
        asum_ref, o_ref = rest
    n2 = a_ref.shape[1]
    cw = a_ref.shape[2]
    a = a_ref[...].reshape(2 * n2, cw)
    xk = jnp.dot(t_ref[...], a, preferred_element_type=F32)
    if conv:
        xr, xi = xk[0:n2], xk[n2:]
        gr, gi = g_ref[0], g_ref[1]
        yk = jnp.concatenate([xr * gr - xi * gi, xr * gi + xi * gr], axis=0).astype(BF16)
        bk = jnp.dot(u_ref[...], yk, preferred_element_type=F32)
        o_ref[...] = bk.reshape(2, n2, cw).astype(o_ref.dtype)
    else:
        xk = xk / (asum_ref[...] + 1e-6)
        o_ref[...] = xk.reshape(2, n2, cw).astype(o_ref.dtype)


def _mid_spectrum(a5, tmat, asum):
    _, _, n1, n2, cw = a5.shape
    return pl.pallas_call(
        functools.partial(_mid_kernel, conv=False),
        out_shape=jax.ShapeDtypeStruct((n1, 2, n2, cw), F32),
        grid=(n1,),
        in_specs=[pl.BlockSpec((None, 2, None, n2, cw), lambda k: (0, 0, k, 0, 0)),
                  pl.BlockSpec((None, 2 * n2, 2 * n2), lambda k: (k, 0, 0)),
                  pl.BlockSpec((1, cw), lambda k: (0, 0))],
        out_specs=pl.BlockSpec((None, 2, n2, cw), lambda k: (k, 0, 0, 0)),
        compiler_params=_cparams(("parallel",)),
        name="hyena_spectrum",
    )(a5, tmat, asum)


def _mid_conv(a5, tmat, umat, gspec, order):
    nb, _, n1, n2, cw = a5.shape
    return pl.pallas_call(
        functools.partial(_mid_kernel, conv=True),
        out_shape=jax.ShapeDtypeStruct((nb, 2, n1, n2, cw), BF16),
        grid=(n1, nb),
        in_specs=[pl.BlockSpec((None, 2, None, n2, cw), lambda k, b: (b, 0, k, 0, 0)),
                  pl.BlockSpec((None, 2 * n2, 2 * n2), lambda k, b: (k, 0, 0)),
                  pl.BlockSpec((None, 2, n2, cw), lambda k, b: (k, 0, 0, order)),
                  pl.BlockSpec((None, 2 * n2, 2 * n2), lambda k, b: (k, 0, 0))],
        out_specs=pl.BlockSpec((None, 2, None, n2, cw), lambda k, b: (b, 0, k, 0, 0)),
        compiler_params=_cparams(("parallel", "arbitrary")),
        name="hyena_mid",
    )(a5, tmat, gspec, umat)


def _dft_tables(seq):
    n = 2 * seq
    n2 = DFT_N2
    n1 = n // n2
    two_pi = 2.0 * math.pi

    def cs(num, den):
        ang = (num % den).astype(F32) * (two_pi / den)
        return jnp.cos(ang), jnp.sin(ang)

    k1 = jnp.arange(n1, dtype=jnp.int32)
    c, s = cs(k1[:, None] * jnp.arange(n1, dtype=jnp.int32)[None, :], n1)
    f1_full = jnp.concatenate([c, -s], axis=0)
    i2 = jnp.arange(n2, dtype=jnp.int32)
    num = i2[None, None, :] * k1[:, None, None] + n1 * (i2[None, :, None] * i2[None, None, :])
    c, s = cs(num, n)
    tre, tim = c, -s
    tmat = jnp.concatenate([jnp.concatenate([tre, -tim], axis=2),
                            jnp.concatenate([tim, tre], axis=2)], axis=1)
    ure, uim = jnp.swapaxes(tre, 1, 2), -jnp.swapaxes(tim, 1, 2)
    umat = jnp.concatenate([jnp.concatenate([ure, -uim], axis=2),
                            jnp.concatenate([uim, ure], axis=2)], axis=1)
    c, s = cs(jnp.arange(n1 // 2, dtype=jnp.int32)[:, None] * k1[None, :], n1)
    fi = jnp.concatenate([c, -s], axis=1) * (1.0 / n)
    return (f1_full.astype(BF16), f1_full[:, :n1 // 2].astype(BF16), tmat.astype(BF16), umat.astype(BF16),
            fi.astype(BF16))


def _dft_tables_small(seq):
    n = 2 * seq
    k = jnp.arange(n, dtype=jnp.int32)
    ang = ((k[:, None] * k[None, :]) % n).astype(F32) * (2.0 * math.pi / n)
    c, s = jnp.cos(ang), jnp.sin(ang)
    fwd = jnp.concatenate([c, -s], axis=0)
    inv = jnp.concatenate([c[:seq], -s[:seq]], axis=1) * (1.0 / n)
    return fwd.astype(BF16), fwd[:, :seq].astype(BF16), inv.astype(BF16)


def _ctxconv_kernel(z_ref, xg_ref, taps_ref, asum_ref, bias_ref, ff_ref, fh_ref, fi_ref, o_ref):
    n = ff_ref.shape[1]
    z = z_ref[...]
    g = jnp.dot(ff_ref[...], taps_ref[...].astype(BF16), preferred_element_type=F32) / (asum_ref[...] + 1e-6)
    xk = jnp.dot(fh_ref[...], z, preferred_element_type=F32)
    xr, xi = xk[0:n], xk[n:]
    gr, gi = g[0:n], g[n:]
    yk = jnp.concatenate([xr * gr - xi * gi, xr * gi + xi * gr], axis=0).astype(BF16)
    y = jnp.dot(fi_ref[...], yk, preferred_element_type=F32)
    o_ref[...] = (xg_ref[...].astype(F32) * (y + bias_ref[...] * z.astype(F32))).astype(o_ref.dtype)


def _ctx_longconv(z, xg, taps, asum, bias, tabs, order, *, bsz, seq):
    ff, fh, fi = tabs
    n = 2 * seq
    cw = BRANCH_W
    return pl.pallas_call(
        _ctxconv_kernel,
        out_shape=jax.ShapeDtypeStruct((bsz * seq, cw), BF16),
        grid=(bsz,),
        in_specs=[pl.BlockSpec((seq, cw), lambda b: (b, 0)),
                  pl.BlockSpec((seq, cw), lambda b: (b, 0)),
                  pl.BlockSpec((n, cw), lambda b: (0, order)),
                  pl.BlockSpec((1, cw), lambda b: (0, order)),
                  pl.BlockSpec((1, cw), lambda b: (0, 0)),
                  pl.BlockSpec((2 * n, n), lambda b: (0, 0)),
                  pl.BlockSpec((2 * n, seq), lambda b: (0, 0)),
                  pl.BlockSpec((seq, 2 * n), lambda b: (0, 0))],
        out_specs=pl.BlockSpec((seq, cw), lambda b: (b, 0)),
        compiler_params=_cparams(("parallel",)),
        name="hyena_ctx_conv",
    )(z, xg, taps, asum, bias, ff, fh, fi)


def _hyena_long(v0, x1, x2, taps, asum, hy_bias, tabs, *, bsz, seq):
    f1_full, f1_half, tmat, umat, fi = tabs
    n2 = DFT_N2
    n1 = 2 * seq // n2
    cw = BRANCH_W
    tn = 4096
    at = _lmm(f1_full, taps.reshape(1, n1, n2 * 2 * cw), tn=tn, out_dtype=BF16, name="taps_dft1")
    gspec = _mid_spectrum(at.reshape(1, 2, n1, n2, 2 * cw), tmat, asum)

    def conv(z, xg, order):
        z2 = z.reshape(bsz, n1 // 2, n2 * cw)
        a = _lmm(f1_half, z2, tn=tn, out_dtype=BF16, name="hyena_dft1")
        bk = _mid_conv(a.reshape(bsz, 2, n1, n2, cw), tmat, umat, gspec, order)
        bias_t = jnp.tile(hy_bias[order].reshape(1, cw), (1, tn // cw))
        y = _lmm(fi, bk.reshape(bsz, 2 * n1, n2 * cw), tn=tn, out_dtype=BF16,
                 gate_args=(xg.reshape(bsz, n1 // 2, n2 * cw), z2, bias_t), name="hyena_idft2")
        return y.reshape(bsz * seq, cw)

    return conv(conv(v0, x1, 0), x2, 1)


def _s5_kernel(u_ref, bre_ref, bim_ref, cre_ref, cim_ref, are_ref, aim_ref, s0_ref,
               y_ref, sfin_ref,
               uf, lhs, bur, bui, ybuf, pw_r, pw_i, car_r, car_i, st_r, st_i, *, tseg, rev):
    i = pl.program_id(1)
    sw = S5_SW
    gw = sw // S5_SUPER
    a_re = are_ref[...]
    a_im = aim_ref[...]

    @pl.when(jnp.logical_and(pl.program_id(0) == 0, i == 0))
    def _():
        def pbody(r, carry):
            cr, ci = carry
            pw_r[pl.ds(r, 1), :] = cr
            pw_i[pl.ds(r, 1), :] = ci
            return cr * a_re - ci * a_im, cr * a_im + ci * a_re
        lax.fori_loop(0, tseg, pbody, (a_re, a_im))

    @pl.when(i == 0)
    def _():
        st_r[...] = s0_ref[:, 0:sw]
        st_i[...] = s0_ref[:, sw:]

    for sg in range(S5_SUPER):
        uf[sg] = u_ref[:, sg * 128:(sg + 1) * 128].astype(F32)

    def gather(r, _):
        for sg in range(S5_SUPER):
            lhs[sg, pl.ds(pl.multiple_of(r * 8, 8), 8), :] = uf[sg, pl.ds(r, 8, stride=tseg), :]
        return 0
    lax.fori_loop(0, tseg, gather, 0)

    for sg in range(S5_SUPER):
        lb = lhs[sg].astype(BF16)
        bur[:, sg * gw:(sg + 1) * gw] = jnp.dot(lb, bre_ref[sg], preferred_element_type=F32)
        bui[:, sg * gw:(sg + 1) * gw] = jnp.dot(lb, bim_ref[sg], preferred_element_type=F32)

    for sg in range(S5_SUPER):
        cols = slice(sg * gw, (sg + 1) * gw)
        ar = jnp.broadcast_to(a_re[:, cols], (8, gw))
        ai = jnp.broadcast_to(a_im[:, cols], (8, gw))

        def sbody(k, carry, cols=cols, ar=ar, ai=ai):
            sr, si = carry
            r = (tseg - 1 - k) if rev else k
            rows = pl.ds(pl.multiple_of(r * 8, 8), 8)
            nr = ar * sr - ai * si + bur[rows, cols]
            ni = ar * si + ai * sr + bui[rows, cols]
            bur[rows, cols] = nr
            bui[rows, cols] = ni
            return nr, ni
        zero = jnp.zeros((8, gw), F32)
        lax.fori_loop(0, tseg, sbody, (zero, zero))

    at_r = pw_r[tseg - 1:tseg, :]
    at_i = pw_i[tseg - 1:tseg, :]
    end_row = 0 if rev else (tseg - 1) * 8
    cr = st_r[...]
    ci = st_i[...]
    order = range(7, -1, -1) if rev else range(8)
    for s in order:
        car_r[s:s + 1, :] = cr
        car_i[s:s + 1, :] = ci
        er = bur[end_row + s:end_row + s + 1, :]
        ei = bui[end_row + s:end_row + s + 1, :]
        cr, ci = er + at_r * cr - at_i * ci, ei + at_r * ci + at_i * cr
    st_r[...] = cr
    st_i[...] = ci
    sfin_ref[:, 0:sw] = cr
    sfin_ref[:, sw:] = ci

    for sg in range(S5_SUPER):
        cols = slice(sg * gw, (sg + 1) * gw)
        kr = car_r[:, cols]
        kim = car_i[:, cols]

        def fbody(r, _, cols=cols, kr=kr, kim=kim):
            pidx = (tseg - 1 - r) if rev else r
            pr = pw_r[pl.ds(pidx, 1), cols]
            pi_ = pw_i[pl.ds(pidx, 1), cols]
            rows = pl.ds(pl.multiple_of(r * 8, 8), 8)
            bur[rows, cols] = bur[rows, cols] + (pr * kr - pi_ * kim)
            bui[rows, cols] = bui[rows, cols] + (pr * kim + pi_ * kr)
            return 0
        lax.fori_loop(0, tseg, fbody, 0)

    for sg in range(S5_SUPER):
        cols = slice(sg * gw, (sg + 1) * gw)
        ybuf[sg] = (jnp.dot(bur[:, cols].astype(BF16), cre_ref[sg], preferred_element_type=F32)
                    + jnp.dot(bui[:, cols].astype(BF16), cim_ref[sg], preferred_element_type=F32))

    def scatter(r, _):
        for sg in range(S5_SUPER):
            y_ref[sg, pl.ds(r, 8, stride=tseg), :] = ybuf[sg, pl.ds(pl.multiple_of(r * 8, 8), 8), :]
        return 0
    lax.fori_loop(0, tseg, scatter, 0)


def _s5_scan(proj, s5p, s0, *, bsz, seq, tseg, rev):
    bre, bim, cre, cim, are, aim = s5p
    tr = 8 * tseg
    nt = seq // tr
    sw = S5_SW
    gw = sw // S5_SUPER
    tile = (lambda b, i: (b * nt + (nt - 1 - i), U_U)) if rev else (lambda b, i: (b * nt + i, U_U))
    otile = (lambda b, i: (0, b * nt + (nt - 1 - i), 0)) if rev else (lambda b, i: (0, b * nt + i, 0))
    full = lambda shape: pl.BlockSpec(shape, lambda b, i: tuple(0 for _ in shape))
    return pl.pallas_call(
        functools.partial(_s5_kernel, tseg=tseg, rev=rev),
        out_shape=(jax.ShapeDtypeStruct((S5_SUPER, bsz * seq, 128), F32),
                   jax.ShapeDtypeStruct((bsz, 1, 2 * sw), F32)),
        grid=(bsz, nt),
        in_specs=[pl.BlockSpec((tr, UNIT), tile),
                  full((S5_SUPER, 128, gw)), full((S5_SUPER, 128, gw)),
                  full((S5_SUPER, gw, 128)), full((S5_SUPER, gw, 128)),
                  full((1, sw)), full((1, sw)),
                  pl.BlockSpec((None, 1, 2 * sw), lambda b, i: (b, 0, 0))],
        out_specs=(pl.BlockSpec((S5_SUPER, tr, 128), otile),
                   pl.BlockSpec((None, 1, 2 * sw), lambda b, i: (b, 0, 0))),
        scratch_shapes=[pltpu.VMEM((S5_SUPER, tr, 128), F32), pltpu.VMEM((S5_SUPER, tr, 128), F32),
                        pltpu.VMEM((tr, sw), F32), pltpu.VMEM((tr, sw), F32),
                        pltpu.VMEM((S5_SUPER, tr, 128), F32),
                        pltpu.VMEM((tseg, sw), F32), pltpu.VMEM((tseg, sw), F32),
                        pltpu.VMEM((8, sw), F32), pltpu.VMEM((8, sw), F32),
                        pltpu.VMEM((1, sw), F32), pltpu.VMEM((1, sw), F32)],
        compiler_params=_cparams(("arbitrary", "arbitrary")),
        name="s5_bwd" if rev else "s5_fwd",
    )(proj, bre, bim, cre, cim, are, aim, s0)


def _s5_params(a_re, a_im, log_step, b_re, b_im, c_re, c_im):
    g, p, ci = S5_GROUPS, S5_STATE, S5_GROUP
    dt = jnp.exp(log_step)[:, None]
    mag = jnp.exp(a_re * dt)
    ar, ai = mag * jnp.cos(a_im * dt), mag * jnp.sin(a_im * dt)
    den = a_re * a_re + a_im * a_im
    fr = ((ar - 1.0) * a_re + ai * a_im) / den
    fi = (ai * a_re - (ar - 1.0) * a_im) / den
    bbr = fr[..., None] * b_re - fi[..., None] * b_im
    bbi = fr[..., None] * b_im + fi[..., None] * b_re
    eye = jnp.eye(8, dtype=F32)

    def blockdiag_b(m):
        m4 = m.reshape(S5_SUPER, 8, p, ci)
        return jnp.einsum('sgpc,gh->sgchp', m4, eye).reshape(S5_SUPER, 8 * ci, 8 * p).astype(BF16)

    def blockdiag_c(m):
        m4 = m.reshape(S5_SUPER, 8, ci, p)
        return jnp.einsum('sgcp,gh->sgphc', m4, eye).reshape(S5_SUPER, 8 * p, 8 * ci).astype(BF16)

    return (blockdiag_b(bbr), blockdiag_b(bbi), blockdiag_c(c_re), blockdiag_c(-c_im),
            ar.reshape(1, g * p), ai.reshape(1, g * p))


def _merge_kernel(ya_ref, yh_ref, hg_ref, yatt_ref, attg_ref, ysf_ref, ysb_ref, u_ref, s5g_ref,
                  mg0_ref, mg1_ref, mg2_ref, mg3_ref, x_ref, mod_ref, s5d_ref, wglu_ref, wb_ref, wout_ref,
                  lng_ref, lnb_ref, o_ref, *, alpha):
    f = lambda r: r[...].astype(F32)
    y_a = f(ya_ref)
    y_h = f(yh_ref) * _silu(f(hg_ref))
    y_c = f(yatt_ref) * _silu(f(attg_ref))
    ys = jnp.concatenate([ysf_ref[sg] + ysb_ref[sg] for sg in range(S5_SUPER)], axis=1)
    y = ys + s5d_ref[...] * f(u_ref)
    zg = jax.nn.gelu(y)
    glu = jnp.dot(zg.astype(BF16), wglu_ref[...], preferred_element_type=F32)
    y_d = zg * jax.nn.sigmoid(glu) * _silu(f(s5g_ref))
    mix = None
    for n, (yn, mg) in enumerate(((y_a, mg0_ref), (y_h, mg1_ref), (y_c, mg2_ref), (y_d, mg3_ref))):
        term = jax.nn.sigmoid(f(mg)) * jnp.dot(yn.astype(BF16), wb_ref[n], preferred_element_type=F32)
        mix = term if mix is None else mix + term
    out = jnp.dot(mix.astype(BF16), wout_ref[...], preferred_element_type=F32)
    gate = mod_ref[2:3, :]
    r = alpha * x_ref[...] + gate * out
    mu = jnp.mean(r, axis=1, keepdims=True)
    rc = r - mu
    var = jnp.mean(rc * rc, axis=1, keepdims=True)
    o_ref[...] = rc * lax.rsqrt(var + LN_EPS) * lng_ref[...] + lnb_ref[...]


def _merge(ya, yh, yatt, ysf, ysb, proj, x2d, mod3, s5d, wglu, wb, wout, lng, lnb, *, seq, tm, alpha):
    m = x2d.shape[0]
    nt_seq = seq // tm
    nb = mod3.shape[0]
    mod_map = (lambda i: (i // nt_seq, 0, 0)) if nb > 1 else (lambda i: (0, 0, 0))
    row = lambda w_: pl.BlockSpec((tm, w_), lambda i: (i, 0))
    pcol = lambda unit: pl.BlockSpec((tm, UNIT), lambda i: (i, unit))
    mcol = lambda n: pl.BlockSpec((tm, 2 * UNIT), lambda i: (i, U_MERGE // 2 + n))
    full = lambda shape: pl.BlockSpec(shape, lambda i: tuple(0 for _ in shape))
    w_ = BRANCH_W
    s5row = pl.BlockSpec((S5_SUPER, tm, 128), lambda i: (0, i, 0))
    return pl.pallas_call(
        functools.partial(_merge_kernel, alpha=alpha),
        out_shape=jax.ShapeDtypeStruct((m, D_MODEL), F32),
        grid=(m // tm,),
        in_specs=[row(w_), row(w_), pcol(U_HG), row(w_), pcol(U_ATTG), s5row, s5row, pcol(U_U), pcol(U_S5G),
                  mcol(0), mcol(1), mcol(2), mcol(3), row(D_MODEL),
                  pl.BlockSpec((None, 3, D_MODEL), mod_map),
                  full((1, w_)), full((w_, w_)), full((N_BRANCH, w_, D_MODEL)), full((D_MODEL, D_MODEL)),
                  full((1, D_MODEL)), full((1, D_MODEL))],
        out_specs=row(D_MODEL),
        compiler_params=_cparams(("parallel",)),
        name="merge_out_norm",
    )(ya, yh, proj, yatt, proj, ysf, ysb, proj, proj, proj, proj, proj, proj, x2d, mod3,
      s5d, wglu, wb, wout, lng, lnb)


def _rope_tables(n_lat):
    rows = n_lat // GRID_W
    row = jnp.broadcast_to(jnp.arange(rows)[:, None], (rows, GRID_W)).reshape(-1)
    col = jnp.broadcast_to(jnp.arange(GRID_W)[None, :], (rows, GRID_W)).reshape(-1)
    half = HEAD_DIM // 2
    inv = 1.0 / (ROPE_BASE ** (jnp.arange(0, half, 2, dtype=F32) / half))
    ar, ac = row[:, None] * inv, col[:, None] * inv
    cos64 = jnp.concatenate([jnp.cos(ar), jnp.cos(ar), jnp.cos(ac), jnp.cos(ac)], axis=-1)
    sin64 = jnp.concatenate([-jnp.sin(ar), jnp.sin(ar), -jnp.sin(ac), jnp.sin(ac)], axis=-1)
    return jnp.tile(cos64, (1, 2)), jnp.tile(sin64, (1, 2))


def _pick(n, prefs):
    for t in prefs:
        if n % t == 0:
            return t
    return n


def kernel(x, c, ctx, c_ctx, w_mod, b_mod, w_in, conv_a, conv_h, hy_w1, hy_b1, hy_w2, hy_b2, hy_w3, hy_freq,
           hy_delta, hy_bias, lam_q1, lam_k1, lam_q2, lam_k2, attn_norm_g, s5_a_re, s5_a_im, s5_log_step,
           s5_b_re, s5_b_im, s5_c_re, s5_c_im, s5_d, s5_w_glu, w_branch, w_out, ln_g, ln_b):
    bsz, seq, d = x.shape
    seq_c = ctx.shape[1]
    depth = w_in.shape[0]
    alpha = (2.0 * depth) ** 0.25
    assert d == D_MODEL and seq % 1024 == 0 and seq_c % 256 == 0 and bsz + 1 <= 8

    cos_t, sin_t = _rope_tables(seq)
    dft_lat = _dft_tables(seq)
    dft_ctx = _dft_tables_small(seq_c)
    cvec = jnp.zeros((8, d), F32).at[0:bsz].set(c).at[bsz].set(c_ctx)
    tm_in = _pick(seq, (1024,))
    tm_el = _pick(seq, (512,))
    tq = _pick(seq, (256,))
    tk = _pick(seq, (1024,))
    tseg_lat = 128
    tseg_ctx = seq_c // 8

    x2 = x.reshape(bsz * seq, d)
    xc2 = ctx.reshape(bsz * seq_c, d)
    for l in range(depth):
        last = l == depth - 1
        lam_init = 0.8 - 0.6 * math.exp(-0.3 * l)
        mod = _modulation(cvec, w_mod[l], b_mod[l]).reshape(8, 3, d)
        mod_lat, mod_ctx = mod[0:bsz], mod[bsz:bsz + 1]
        w_l = w_in[l].reshape(d, N_UNITS, UNIT)[:, jnp.array(UNIT_PERM)].reshape(d, PROJ_W).astype(BF16)
        lamp = jnp.stack([lam_q1[l], lam_k1[l], lam_q2[l], lam_k2[l]], axis=0)
        g_att = attn_norm_g[l].reshape(1, V_DIM)
        s5f = _s5_params(s5_a_re[l, 0], s5_a_im[l, 0], s5_log_step[l, 0], s5_b_re[l, 0], s5_b_im[l, 0],
                         s5_c_re[l, 0], s5_c_im[l, 0])
        s5b = _s5_params(s5_a_re[l, 1], s5_a_im[l, 1], s5_log_step[l, 1], s5_b_re[l, 1], s5_b_im[l, 1],
                         s5_c_re[l, 1], s5_c_im[l, 1])
        hp = (hy_w1[l], hy_b1[l], hy_w2[l], hy_b2[l], hy_w3[l], hy_freq[l], hy_delta[l])
        wglu = s5_w_glu[l].astype(BF16)
        wb = w_branch[l].astype(BF16)
        wout = w_out[l].astype(BF16)
        s5d = s5_d[l].reshape(1, BRANCH_W)
        lng, lnb = ln_g[l].reshape(1, d), ln_b[l].reshape(1, d)

        ncols_c = 3 * 1024 if last else PROJ_W
        projc = _inproj(xc2, mod_ctx, w_l[:, :ncols_c], cos_t, sin_t, seq=seq_c, rope=False,
                        tm=bsz * seq_c, ncols=ncols_c)
        zero_state = jnp.zeros((bsz, 1, 2 * S5_SW), F32)
        ycf, scf = _s5_scan(projc, s5f, zero_state, bsz=bsz, seq=seq_c, tseg=tseg_ctx, rev=False)
        ycb, scb = _s5_scan(projc, s5b, zero_state, bsz=bsz, seq=seq_c, tseg=tseg_ctx, rev=True)

        proj = _inproj(x2, mod_lat, w_l, cos_t, sin_t, seq=seq, rope=True, tm=tm_in, ncols=PROJ_W)
        yatt = _attention(proj, proj, projc, lamp, g_att, bsz=bsz, seq_q=seq, seq_kv=seq, seq_c=seq_c,
                          lam_init=lam_init, tq=tq, tk=tk)
        ysf, _ = _s5_scan(proj, s5f, scf, bsz=bsz, seq=seq, tseg=tseg_lat, rev=False)
        ysb, _ = _s5_scan(proj, s5b, scb, bsz=bsz, seq=seq, tseg=tseg_lat, rev=True)
        ya, v0, x1, x2h = _convgate(proj, conv_a[l], conv_h[l], bsz=bsz, seq=seq, tm=tm_el)
        taps, asum = _hyena_taps(hp, seq=seq, tr=512)
        yh = _hyena_long(v0, x1, x2h, taps, asum, hy_bias[l], dft_lat, bsz=bsz, seq=seq)
        x_new = _merge(ya, yh, yatt, ysf, ysb, proj, x2, mod_lat, s5d, wglu, wb, wout, lng, lnb,
                       seq=seq, tm=tm_el, alpha=alpha)

        if not last:
            yatt_c = _attention(projc, None, projc, lamp, g_att, bsz=bsz, seq_q=seq_c, seq_kv=0, seq_c=seq_c,
                                lam_init=lam_init, tq=seq_c, tk=seq_c)
            ya_c, v0c, x1c, x2c = _convgate(projc, conv_a[l], conv_h[l], bsz=bsz, seq=seq_c, tm=seq_c)
            taps_c, asum_c = _hyena_taps(hp, seq=seq_c, tr=seq_c)
            b0 = hy_bias[l, 0].reshape(1, BRANCH_W)
            b1 = hy_bias[l, 1].reshape(1, BRANCH_W)
            z1c = _ctx_longconv(v0c, x1c, taps_c, asum_c, b0, dft_ctx, 0, bsz=bsz, seq=seq_c)
            yh_c = _ctx_longconv(z1c, x2c, taps_c, asum_c, b1, dft_ctx, 1, bsz=bsz, seq=seq_c)
            xc2 = _merge(ya_c, yh_c, yatt_c, ycf, ycb, projc, xc2, mod_ctx, s5d, wglu, wb, wout, lng, lnb,
                         seq=seq_c, tm=seq_c, alpha=alpha)
        x2 = x_new
    return x2.reshape(bsz, seq, d)
```

```python
import functools
import math

import jax
import jax.numpy as jnp
from jax import lax
from jax.experimental import pallas as pl
from jax.experimental.pallas import tpu as pltpu

F32 = jnp.float32
BF16 = jnp.bfloat16
HIGHEST = lax.Precision.HIGHEST

D_MODEL = 1024
BRANCH_W = 512
N_HEADS = 4
HEAD_DIM = 64
V_DIM = 128
GRID_W = 64
ROPE_BASE = 10000.0
HYENA_BANDS = 16
HYENA_HIDDEN = 64
HYENA_SHIFT = 0.05
S5_GROUP = 16
S5_GROUPS = 32
S5_STATE = 64
S5_SUPER = 4
S5_SW = S5_GROUPS * S5_STATE
LN_EPS = 1e-5
N_BRANCH = 4
UNIT = 512
N_UNITS = 22
PROJ_W = N_UNITS * UNIT
UNIT_PERM = (0, 1, 11, 12, 2, 4, 5, 7, 8, 9, 3, 6, 13, 10, 14, 15, 16, 17, 18, 19, 20, 21)
U_K, U_V, U_Q, U_ATTG, U_U, U_CONV, U_AB, U_S5G, U_HG, U_MERGE = 0, 1, 2, 3, 4, 5, 10, 12, 13, 14
DFT_N2 = 128
MID_K1_PER_STEP = 4
VMEM_LIMIT = 56 * 1024 * 1024


def _cparams(sem):
    return pltpu.CompilerParams(dimension_semantics=sem, vmem_limit_bytes=VMEM_LIMIT)


def _silu(v):
    return v * jax.nn.sigmoid(v)


def _mod_kernel(s_ref, w_ref, b_ref, o_ref):
    s = _silu(s_ref[...])
    o_ref[...] = jnp.dot(s, w_ref[...], preferred_element_type=F32, precision=HIGHEST) + b_ref[...]


def _modulation(cvec, w, b):
    n = w.shape[1]
    tn = 512
    return pl.pallas_call(
        _mod_kernel,
        out_shape=jax.ShapeDtypeStruct((8, n), F32),
        grid=(n // tn,),
        in_specs=[pl.BlockSpec((8, D_MODEL), lambda j: (0, 0)),
                  pl.BlockSpec((D_MODEL, tn), lambda j: (0, j)),
                  pl.BlockSpec((1, tn), lambda j: (0, j))],
        out_specs=pl.BlockSpec((8, tn), lambda j: (0, j)),
        compiler_params=_cparams(("parallel",)),
        name="modulation",
    )(cvec, w, b.reshape(1, n))


def _inproj_kernel(x_ref, mod_ref, w_ref, cos_ref, sin_ref, o_ref, h_ref, *, rope):
    j = pl.program_id(1)

    @pl.when(j == 0)
    def _():
        shift = mod_ref[0:1, :]
        scale = mod_ref[1:2, :]
        h_ref[...] = (x_ref[...] * (1.0 + scale) + shift).astype(BF16)

    acc = jnp.dot(h_ref[...], w_ref[...], preferred_element_type=F32)
    if not rope:
        o_ref[...] = acc.astype(o_ref.dtype)
        return

    @pl.when(j < 2)
    def _():
        cs = cos_ref[...]
        sn = sin_ref[...]
        lane = lax.broadcasted_iota(jnp.int32, cs.shape, 1)
        first = (lane % 32) < 16
        for cb in range(UNIT // 128):
            t = acc[:, cb * 128:(cb + 1) * 128]
            partner = jnp.where(first, pltpu.roll(t, 128 - 16, 1), pltpu.roll(t, 16, 1))
            o_ref[:, cb * 128:(cb + 1) * 128] = (t * cs + partner * sn).astype(o_ref.dtype)
        o_ref[:, UNIT:] = acc[:, UNIT:].astype(o_ref.dtype)

    @pl.when(j >= 2)
    def _():
        o_ref[...] = acc.astype(o_ref.dtype)


def _inproj(x2d, mod3, w, cos_t, sin_t, *, seq, rope, tm, ncols):
    m = x2d.shape[0]
    tn = 1024
    nt_seq = max(seq // tm, 1)
    nb = mod3.shape[0]
    mod_map = (lambda i, j: (i // nt_seq, 0, 0)) if nb > 1 else (lambda i, j: (0, 0, 0))
    return pl.pallas_call(
        functools.partial(_inproj_kernel, rope=rope),
        out_shape=jax.ShapeDtypeStruct((m, ncols), BF16),
        grid=(m // tm, ncols // tn),
        in_specs=[pl.BlockSpec((tm, D_MODEL), lambda i, j: (i, 0)),
                  pl.BlockSpec((None, 3, D_MODEL), mod_map),
                  pl.BlockSpec((D_MODEL, tn), lambda i, j: (0, j)),
                  pl.BlockSpec((tm, 128), lambda i, j: (i % nt_seq, 0)),
                  pl.BlockSpec((tm, 128), lambda i, j: (i % nt_seq, 0))],
        out_specs=pl.BlockSpec((tm, tn), lambda i, j: (i, j)),
        scratch_shapes=[pltpu.VMEM((tm, D_MODEL), BF16)],
        compiler_params=_cparams(("parallel", "arbitrary")),
        name="inproj_rope" if rope else "inproj_ctx",
    )(x2d, mod3, w, cos_t, sin_t)


def _attn_kernel(*refs, n_lat, tq, lam_init):
    if n_lat:
        q_ref, k_ref, v_ref, kc_ref, vc_ref, lamp_ref, g_ref, o_ref, qz, m_s, l_s, acc = refs
    else:
        q_ref, kc_ref, vc_ref, lamp_ref, g_ref, o_ref, qz, m_s, l_s, acc = refs
    ki = pl.program_id(3)

    @pl.when(ki == 0)
    def _():
        q = q_ref[...].astype(F32) * (HEAD_DIM ** -0.5)
        lane = lax.broadcasted_iota(jnp.int32, q.shape, 1)
        qz[0:tq, :] = jnp.where(lane < HEAD_DIM, q, 0.0).astype(BF16)
        qz[tq:, :] = jnp.where(lane >= HEAD_DIM, q, 0.0).astype(BF16)
        m_s[...] = jnp.full(m_s.shape, -jnp.inf, F32)
        l_s[...] = jnp.zeros(l_s.shape, F32)
        acc[...] = jnp.zeros(acc.shape, F32)

    def update(k, v):
        s = lax.dot_general(qz[...], k, (((1,), (1,)), ((), ())), preferred_element_type=F32)
        m_prev = m_s[...]
        m_new = jnp.maximum(m_prev, jnp.max(s, axis=1, keepdims=True))
        alpha = jnp.exp(m_prev - m_new)
        p = jnp.exp(s - m_new)
        l_s[...] = alpha * l_s[...] + jnp.sum(p, axis=1, keepdims=True)
        acc[...] = alpha * acc[...] + jnp.dot(p.astype(BF16), v, preferred_element_type=F32)
        m_s[...] = m_new

    if n_lat:
        @pl.when(ki < n_lat)
        def _():
            update(k_ref[...], v_ref[...])

        @pl.when(ki == n_lat)
        def _():
            update(kc_ref[...], vc_ref[...])
    else:
        update(kc_ref[...], vc_ref[...])

    @pl.when(ki == n_lat)
    def _():
        o = acc[...] / l_s[...]
        lp = lamp_ref[...]
        lam = (jnp.exp(jnp.sum(lp[0:1] * lp[1:2], axis=1, keepdims=True))
               - jnp.exp(jnp.sum(lp[2:3] * lp[3:4], axis=1, keepdims=True)) + lam_init)
        od = o[0:tq] - lam * o[tq:]
        od = od * lax.rsqrt(jnp.mean(od * od, axis=1, keepdims=True) + 1e-5) * g_ref[...] * (1.0 - lam_init)
        o_ref[...] = od.astype(o_ref.dtype)


def _attention(proj_q, proj_kv, proj_c, lamp, g, *, bsz, seq_q, seq_kv, seq_c, lam_init, tq, tk):
    nq = seq_q // tq
    n_lat = 0 if proj_kv is None else seq_kv // tk
    hq = U_Q * UNIT // 128
    hk = U_K * UNIT // 128
    hv = U_V * UNIT // 128
    in_specs = [pl.BlockSpec((tq, 128), lambda b, h, qi, ki: (b * nq + qi, hq + h))]
    args = [proj_q]
    if n_lat:
        kmap = lambda b, h, qi, ki: (b * n_lat + jnp.minimum(ki, n_lat - 1), hk + h)
        vmap_ = lambda b, h, qi, ki: (b * n_lat + jnp.minimum(ki, n_lat - 1), hv + h)
        in_specs += [pl.BlockSpec((tk, 128), kmap), pl.BlockSpec((tk, 128), vmap_)]
        args += [proj_kv, proj_kv]
    in_specs += [pl.BlockSpec((seq_c, 128), lambda b, h, qi, ki: (b, hk + h)),
                 pl.BlockSpec((seq_c, 128), lambda b, h, qi, ki: (b, hv + h)),
                 pl.BlockSpec((4, HEAD_DIM), lambda b, h, qi, ki: (0, 0)),
                 pl.BlockSpec((1, V_DIM), lambda b, h, qi, ki: (0, 0))]
    args += [proj_c, proj_c, lamp, g]
    return pl.pallas_call(
        functools.partial(_attn_kernel, n_lat=n_lat, tq=tq, lam_init=lam_init),
        out_shape=jax.ShapeDtypeStruct((bsz * seq_q, N_HEADS * V_DIM), BF16),
        grid=(bsz, N_HEADS, nq, n_lat + 1),
        in_specs=in_specs,
        out_specs=pl.BlockSpec((tq, V_DIM), lambda b, h, qi, ki: (b * nq + qi, h)),
        scratch_shapes=[pltpu.VMEM((2 * tq, 128), BF16), pltpu.VMEM((2 * tq, 1), F32),
                        pltpu.VMEM((2 * tq, 1), F32), pltpu.VMEM((2 * tq, V_DIM), F32)],
        compiler_params=_cparams(("parallel", "parallel", "parallel", "arbitrary")),
        name="diff_attn" if n_lat else "diff_attn_ctx",
    )(*args)


def _attn_lat_kernel(q_ref, k_ref, v_ref, kc_ref, vc_ref, lamp_ref, g_ref, o_ref,
                     qz, vext, vcext, sbuf, pbuf, abuf, m_s, acc, *, tq, nsub, ck, n_chunks, lam_init):
    @pl.when(pl.program_id(2) == 0)
    def _():
        vext[:, 0:V_DIM] = v_ref[...]
        vext[:, V_DIM:] = jnp.ones((vext.shape[0], V_DIM), BF16)
        vcext[:, 0:V_DIM] = vc_ref[...]
        vcext[:, V_DIM:] = jnp.ones((vcext.shape[0], V_DIM), BF16)

    subs = range(nsub)
    for h in subs:
        q = q_ref[h * tq:(h + 1) * tq, :].astype(F32) * (HEAD_DIM ** -0.5 * math.log2(math.e))
        lane = lax.broadcasted_iota(jnp.int32, q.shape, 1)
        qz[h, 0:tq, :] = jnp.where(lane < HEAD_DIM, q, 0.0).astype(BF16)
        qz[h, tq:, :] = jnp.where(lane >= HEAD_DIM, q, 0.0).astype(BF16)

    def qk(h, kblk):
        return lax.dot_general(qz[h], kblk, (((1,), (1,)), ((), ())), preferred_element_type=F32)

    def kchunk(c):
        return k_ref[pl.ds(pl.multiple_of(c * ck, ck), ck), :]

    def vchunk(c):
        return vext[pl.ds(pl.multiple_of(c * ck, ck), ck), :]

    def score(slot, c):
        kblk = kchunk(c)
        for h in subs:
            sbuf[h, slot] = qk(h, kblk)

    def softmax(slot):
        for h in subs:
            s = sbuf[h, slot]
            m_prev = m_s[h]
            m_new = jnp.maximum(m_prev, jnp.max(s, axis=1, keepdims=True))
            abuf[h, slot] = jnp.exp2(m_prev - m_new)
            pbuf[h, slot] = jnp.exp2(s - m_new).astype(BF16)
            m_s[h] = m_new

    def pv(slot, c):
        vblk = vchunk(c)
        for h in subs:
            acc[h] = abuf[h, slot] * acc[h] + jnp.dot(pbuf[h, slot], vblk, preferred_element_type=F32)

    for h in subs:
        s = qk(h, kc_ref[...])
        m0 = jnp.max(s, axis=1, keepdims=True)
        m_s[h] = m0
        acc[h] = jnp.dot(jnp.exp2(s - m0).astype(BF16), vcext[...], preferred_element_type=F32)

    score(0, 0)
    softmax(0)
    score(1, 1)

    def body(j, _):
        a = 2 * j
        pv(0, a)
        score(0, a + 2)
        softmax(1)
        pv(1, a + 1)
        score(1, a + 3)
        softmax(0)
        return 0
    lax.fori_loop(0, n_chunks // 2 - 1, body, 0)

    pv(0, n_chunks - 2)
    softmax(1)
    pv(1, n_chunks - 1)

    lp = lamp_ref[...]
    lam = (jnp.exp(jnp.sum(lp[0:1] * lp[1:2], axis=1, keepdims=True))
           - jnp.exp(jnp.sum(lp[2:3] * lp[3:4], axis=1, keepdims=True)) + lam_init)
    for h in subs:
        a_ = acc[h]
        o = a_[:, 0:V_DIM] / a_[:, V_DIM:]
        od = o[0:tq] - lam * o[tq:]
        od = od * lax.rsqrt(jnp.mean(od * od, axis=1, keepdims=True) + 1e-5) * g_ref[...] * (1.0 - lam_init)
        o_ref[h * tq:(h + 1) * tq, :] = od.astype(o_ref.dtype)


def _attention_lat(proj, proj_c, lamp, g, *, bsz, seq, seq_c, lam_init, tq, nsub, ck):
    tqs = tq * nsub
    nq = seq // tqs
    n_chunks = seq // ck
    assert n_chunks % 2 == 0 and n_chunks >= 2
    hq = U_Q * UNIT // 128
    hk = U_K * UNIT // 128
    hv = U_V * UNIT // 128
    return pl.pallas_call(
        functools.partial(_attn_lat_kernel, tq=tq, nsub=nsub, ck=ck, n_chunks=n_chunks, lam_init=lam_init),
        out_shape=jax.ShapeDtypeStruct((bsz * seq, N_HEADS * V_DIM), BF16),
        grid=(bsz, N_HEADS, nq),
        in_specs=[pl.BlockSpec((tqs, 128), lambda b, h, qi: (b * nq + qi, hq + h)),
                  pl.BlockSpec((seq, 128), lambda b, h, qi: (b, hk + h)),
                  pl.BlockSpec((seq, 128), lambda b, h, qi: (b, hv + h)),
                  pl.BlockSpec((seq_c, 128), lambda b, h, qi: (b, hk + h)),
                  pl.BlockSpec((seq_c, 128), lambda b, h, qi: (b, hv + h)),
                  pl.BlockSpec((4, HEAD_DIM), lambda b, h, qi: (0, 0)),
                  pl.BlockSpec((1, V_DIM), lambda b, h, qi: (0, 0))],
        out_specs=pl.BlockSpec((tqs, V_DIM), lambda b, h, qi: (b * nq + qi, h)),
        scratch_shapes=[pltpu.VMEM((nsub, 2 * tq, 128), BF16),
                        pltpu.VMEM((seq, 2 * V_DIM), BF16), pltpu.VMEM((seq_c, 2 * V_DIM), BF16),
                        pltpu.VMEM((nsub, 2, 2 * tq, ck), F32), pltpu.VMEM((nsub, 2, 2 * tq, ck), BF16),
                        pltpu.VMEM((nsub, 2, 2 * tq, 1), F32), pltpu.VMEM((nsub, 2 * tq, 1), F32),
                        pltpu.VMEM((nsub, 2 * tq, 2 * V_DIM), F32)],
        compiler_params=_cparams(("parallel", "parallel", "arbitrary")),
        name="diff_attn",
    )(proj, proj, proj, proj_c, proj_c, lamp, g)


def _conv3(cur, prv, nxt, w):
    tm = cur.shape[0]
    row = lax.broadcasted_iota(jnp.int32, cur.shape, 0)
    dn = jnp.where(row == 0, prv, pltpu.roll(cur, 1, 0))
    up = jnp.where(row == tm - 1, nxt, pltpu.roll(cur, tm - 1, 0))
    return dn * w[0:1] + cur * w[1:2] + up * w[2:3]


def _convgate_kernel(main_ref, prev_ref, next_ref, ab_ref, ca_ref, ch_ref,
                     ya_ref, v0_ref, x1_ref, x2_ref, *, nt):
    i = pl.program_id(1)
    w_ = BRANCH_W
    m = main_ref[...].astype(F32)
    pv = jnp.where(i > 0, prev_ref[15:16, :].astype(F32), 0.0)
    nx = jnp.where(i < nt - 1, next_ref[0:1, :].astype(F32), 0.0)
    ab = ab_ref[...].astype(F32)
    p = m[:, 0:w_] * m[:, w_:2 * w_]
    p_prev = pv[:, 0:w_] * pv[:, w_:2 * w_]
    p_next = nx[:, 0:w_] * nx[:, w_:2 * w_]
    ya = ab[:, 0:w_] * _conv3(p, p_prev, p_next, ca_ref[...]) * _silu(ab[:, w_:])
    ya_ref[...] = ya.astype(ya_ref.dtype)
    ch = ch_ref[...]
    for n, ref in enumerate((v0_ref, x1_ref, x2_ref)):
        lo, hi = (2 + n) * w_, (3 + n) * w_
        ref[...] = _conv3(m[:, lo:hi], pv[:, lo:hi], nx[:, lo:hi], ch[:, n * w_:(n + 1) * w_]).astype(ref.dtype)


def _convgate(proj, conv_a, conv_h, *, bsz, seq, tm):
    m = bsz * seq
    nt = seq // tm
    cw = 5 * UNIT
    hb = 16
    nhb = m // hb
    out = jax.ShapeDtypeStruct((m, BRANCH_W), BF16)
    ospec = pl.BlockSpec((tm, BRANCH_W), lambda b, i: (b * nt + i, 0))
    return pl.pallas_call(
        functools.partial(_convgate_kernel, nt=nt),
        out_shape=(out, out, out, out),
        grid=(bsz, nt),
        in_specs=[pl.BlockSpec((tm, cw), lambda b, i: (b * nt + i, U_CONV * UNIT // cw)),
                  pl.BlockSpec((hb, cw), lambda b, i: (jnp.maximum((b * nt + i) * (tm // hb) - 1, 0), 1)),
                  pl.BlockSpec((hb, cw), lambda b, i: (jnp.minimum((b * nt + i + 1) * (tm // hb), nhb - 1), 1)),
                  pl.BlockSpec((tm, 2 * UNIT), lambda b, i: (b * nt + i, U_AB * UNIT // (2 * UNIT))),
                  pl.BlockSpec((3, BRANCH_W), lambda b, i: (0, 0)),
                  pl.BlockSpec((3, 3 * BRANCH_W), lambda b, i: (0, 0))],
        out_specs=(ospec, ospec, ospec, ospec),
        compiler_params=_cparams(("parallel", "parallel")),
        name="convgate",
    )(proj, proj, proj, proj, conv_a, conv_h)


def _taps_kernel(w1t_ref, w1c_ref, w1s_ref, b1_ref, w2_ref, b2_ref, w3_ref, fr_ref, dl_ref,
                 taps_ref, asum_ref, *, seq, tr):
    i = pl.program_id(0)
    mrow = i * tr + lax.broadcasted_iota(jnp.int32, (tr, 1), 0)
    src = jnp.where(mrow < seq, mrow, 2 * seq - mrow).astype(F32)
    t = src / (seq - 1.0)
    w = (2.0 * math.pi / seq) * src
    band = lax.broadcasted_iota(jnp.int32, (1, HYENA_BANDS), 1).astype(F32)
    f = 1e-4 + band * ((HYENA_BANDS - 1.0 - 1e-4) / (HYENA_BANDS - 1.0))
    ang = w * f
    pre = (t * w1t_ref[...]
           + jnp.dot(jnp.cos(ang), w1c_ref[...], preferred_element_type=F32, precision=HIGHEST)
           + jnp.dot(-jnp.sin(ang), w1s_ref[...], preferred_element_type=F32, precision=HIGHEST)
           + b1_ref[...])
    fr = fr_ref[...]
    h = jnp.sin(fr[0:1] * pre)
    h = jnp.sin(fr[1:2] * (jnp.dot(h, w2_ref[...], preferred_element_type=F32, precision=HIGHEST) + b2_ref[...]))
    h = jnp.dot(h, w3_ref[...], preferred_element_type=F32, precision=HIGHEST)
    decay = jnp.exp(-t * jnp.abs(dl_ref[...]))
    out = h * (decay + HYENA_SHIFT)
    out = jnp.where(mrow == seq, 0.0, out)
    taps_ref[...] = out

    @pl.when(i == 0)
    def _():
        asum_ref[...] = jnp.zeros(asum_ref.shape, F32)

    asum_ref[...] += jnp.sum(jnp.abs(out), axis=0, keepdims=True)


def _hyena_taps(hp, *, seq, tr):
    w1, b1, w2, b2, w3, freq, deltas = hp
    hh = HYENA_HIDDEN
    cw = 2 * BRANCH_W
    w3d = w3.reshape(hh, 2, 2, BRANCH_W).transpose(2, 0, 1, 3).reshape(2, hh, cw)
    dld = deltas.transpose(1, 0, 2).reshape(2, 1, cw)
    nt = 2 * seq // tr
    full = lambda shape: pl.BlockSpec(shape, lambda i: tuple(0 for _ in shape))
    return pl.pallas_call(
        functools.partial(_taps_kernel, seq=seq, tr=tr),
        out_shape=(jax.ShapeDtypeStruct((2 * seq, cw), F32), jax.ShapeDtypeStruct((1, cw), F32)),
        grid=(nt,),
        in_specs=[full((1, hh)), full((HYENA_BANDS, hh)), full((HYENA_BANDS, hh)), full((1, hh)),
                  full((hh, hh)), full((1, hh)),
                  pl.BlockSpec((None, hh, cw), lambda i: (i // (nt // 2), 0, 0)),
                  full((2, hh)),
                  pl.BlockSpec((None, 1, cw), lambda i: (i // (nt // 2), 0, 0))],
        out_specs=(pl.BlockSpec((tr, cw), lambda i: (i, 0)), pl.BlockSpec((1, cw), lambda i: (0, 0))),
        compiler_params=_cparams(("arbitrary",)),
        name="hyena_taps",
    )(w1[0:1], w1[1:1 + HYENA_BANDS], w1[1 + HYENA_BANDS:], b1.reshape(1, hh), w2, b2.reshape(1, hh),
      w3d, freq, dld)


def _lmm_kernel(a_ref, x_ref, *rest, gate):
    if gate:
        xg_ref, z_ref, bias_ref, o_ref = rest
    else:
        (o_ref,) = rest
    acc = jnp.dot(a_ref[...], x_ref[...].astype(BF16), preferred_element_type=F32)
    if gate:
        acc = xg_ref[...].astype(F32) * (acc + bias_ref[...] * z_ref[...].astype(F32))
    o_ref[...] = acc.astype(o_ref.dtype)


def _lmm(a, x, *, tn, out_dtype, gate_args=None, name):
    nb, k, n = x.shape
    ma = a.shape[0]
    in_specs = [pl.BlockSpec((ma, k), lambda b, j: (0, 0)), pl.BlockSpec((None, k, tn), lambda b, j: (b, 0, j))]
    args = [a, x]
    if gate_args is not None:
        xg, z, bias_t = gate_args
        in_specs += [pl.BlockSpec((None, ma, tn), lambda b, j: (b, 0, j)),
                     pl.BlockSpec((None, ma, tn), lambda b, j: (b, 0, j)),
                     pl.BlockSpec((1, tn), lambda b, j: (0, 0))]
        args += [xg, z, bias_t]
    return pl.pallas_call(
        functools.partial(_lmm_kernel, gate=gate_args is not None),
        out_shape=jax.ShapeDtypeStruct((nb, ma, n), out_dtype),
        grid=(nb, n // tn),
        in_specs=in_specs,
        out_specs=pl.BlockSpec((None, ma, tn), lambda b, j: (b, 0, j)),
        compiler_params=_cparams(("parallel", "parallel")),
        name=name,
    )(*args)


def _mid_kernel(a_ref, t_ref, *rest, conv):
    if conv:
        g_ref, u_ref, o_ref = rest
    else:
        asum_ref, o_ref = rest
    kb, n2 = a_ref.shape[1], a_ref.shape[2]
    for kk in range(kb):
        a = jnp.concatenate([a_ref[0, kk], a_ref[1, kk]], axis=0)
        xk = jnp.dot(t_ref[kk], a, preferred_element_type=F32)
        if conv:
            xr, xi = xk[0:n2], xk[n2:]
            gr, gi = g_ref[kk, 0], g_ref[kk, 1]
            yk = jnp.concatenate([xr * gr - xi * gi, xr * gi + xi * gr], axis=0).astype(BF16)
            bk = jnp.dot(u_ref[kk], yk, preferred_element_type=F32)
            o_ref[0, kk] = bk[0:n2].astype(o_ref.dtype)
            o_ref[1, kk] = bk[n2:].astype(o_ref.dtype)
        else:
            xk = xk / (asum_ref[...] + 1e-6)
            o_ref[kk, 0] = xk[0:n2]
            o_ref[kk, 1] = xk[n2:]


def _mid_spectrum(a5, tmat, asum):
    _, _, n1, n2, cw = a5.shape
    kb = _pick(n1, (MID_K1_PER_STEP,))
    return pl.pallas_call(
        functools.partial(_mid_kernel, conv=False),
        out_shape=jax.ShapeDtypeStruct((n1, 2, n2, cw), F32),
        grid=(n1 // kb,),
        in_specs=[pl.BlockSpec((None, 2, kb, n2, cw), lambda k: (0, 0, k, 0, 0)),
                  pl.BlockSpec((kb, 2 * n2, 2 * n2), lambda k: (k, 0, 0)),
                  pl.BlockSpec((1, cw), lambda k: (0, 0))],
        out_specs=pl.BlockSpec((kb, 2, n2, cw), lambda k: (k, 0, 0, 0)),
        compiler_params=_cparams(("parallel",)),
        name="hyena_spectrum",
    )(a5, tmat, asum)


def _mid_conv(a5, tmat, umat, gspec, order):
    nb, _, n1, n2, cw = a5.shape
    kb = _pick(n1, (MID_K1_PER_STEP,))
    return pl.pallas_call(
        functools.partial(_mid_kernel, conv=True),
        out_shape=jax.ShapeDtypeStruct((nb, 2, n1, n2, cw), BF16),
        grid=(n1 // kb, nb),
        in_specs=[pl.BlockSpec((None, 2, kb, n2, cw), lambda k, b: (b, 0, k, 0, 0)),
                  pl.BlockSpec((kb, 2 * n2, 2 * n2), lambda k, b: (k, 0, 0)),
                  pl.BlockSpec((kb, 2, n2, cw), lambda k, b: (k, 0, 0, order)),
                  pl.BlockSpec((kb, 2 * n2, 2 * n2), lambda k, b: (k, 0, 0))],
        out_specs=pl.BlockSpec((None, 2, kb, n2, cw), lambda k, b: (b, 0, k, 0, 0)),
        compiler_params=_cparams(("parallel", "arbitrary")),
        name="hyena_mid",
    )(a5, tmat, gspec, umat)


def _dft_tables(seq):
    n = 2 * seq
    n2 = DFT_N2
    n1 = n // n2
    two_pi = 2.0 * math.pi

    def cs(num, den):
        ang = (num % den).astype(F32) * (two_pi / den)
        return jnp.cos(ang), jnp.sin(ang)

    k1 = jnp.arange(n1, dtype=jnp.int32)
    c, s = cs(k1[:, None] * jnp.arange(n1, dtype=jnp.int32)[None, :], n1)
    f1_full = jnp.concatenate([c, -s], axis=0)
    i2 = jnp.arange(n2, dtype=jnp.int32)
    num = i2[None, None, :] * k1[:, None, None] + n1 * (i2[None, :, None] * i2[None, None, :])
    c, s = cs(num, n)
    tre, tim = c, -s
    tmat = jnp.concatenate([jnp.concatenate([tre, -tim], axis=2),
                            jnp.concatenate([tim, tre], axis=2)], axis=1)
    ure, uim = jnp.swapaxes(tre, 1, 2), -jnp.swapaxes(tim, 1, 2)
    umat = jnp.concatenate([jnp.concatenate([ure, -uim], axis=2),
                            jnp.concatenate([uim, ure], axis=2)], axis=1)
    c, s = cs(jnp.arange(n1 // 2, dtype=jnp.int32)[:, None] * k1[None, :], n1)
    fi = jnp.concatenate([c, -s], axis=1) * (1.0 / n)
    return (f1_full.astype(BF16), f1_full[:, :n1 // 2].astype(BF16), tmat.astype(BF16), umat.astype(BF16),
            fi.astype(BF16))


def _dft_tables_small(seq):
    n = 2 * seq
    k = jnp.arange(n, dtype=jnp.int32)
    ang = ((k[:, None] * k[None, :]) % n).astype(F32) * (2.0 * math.pi / n)
    c, s = jnp.cos(ang), jnp.sin(ang)
    fwd = jnp.concatenate([c, -s], axis=0)
    inv = jnp.concatenate([c[:seq], -s[:seq]], axis=1) * (1.0 / n)
    return fwd.astype(BF16), fwd[:, :seq].astype(BF16), inv.astype(BF16)


def _ctxconv_kernel(z_ref, xg_ref, taps_ref, asum_ref, bias_ref, ff_ref, fh_ref, fi_ref, o_ref):
    n = ff_ref.shape[1]
    z = z_ref[...]
    g = jnp.dot(ff_ref[...], taps_ref[...].astype(BF16), preferred_element_type=F32) / (asum_ref[...] + 1e-6)
    xk = jnp.dot(fh_ref[...], z, preferred_element_type=F32)
    xr, xi = xk[0:n], xk[n:]
    gr, gi = g[0:n], g[n:]
    yk = jnp.concatenate([xr * gr - xi * gi, xr * gi + xi * gr], axis=0).astype(BF16)
    y = jnp.dot(fi_ref[...], yk, preferred_element_type=F32)
    o_ref[...] = (xg_ref[...].astype(F32) * (y + bias_ref[...] * z.astype(F32))).astype(o_ref.dtype)


def _ctx_longconv(z, xg, taps, asum, bias, tabs, order, *, bsz, seq):
    ff, fh, fi = tabs
    n = 2 * seq
    cw = BRANCH_W
    return pl.pallas_call(
        _ctxconv_kernel,
        out_shape=jax.ShapeDtypeStruct((bsz * seq, cw), BF16),
        grid=(bsz,),
        in_specs=[pl.BlockSpec((seq, cw), lambda b: (b, 0)),
                  pl.BlockSpec((seq, cw), lambda b: (b, 0)),
                  pl.BlockSpec((n, cw), lambda b: (0, order)),
                  pl.BlockSpec((1, cw), lambda b: (0, order)),
                  pl.BlockSpec((1, cw), lambda b: (0, 0)),
                  pl.BlockSpec((2 * n, n), lambda b: (0, 0)),
                  pl.BlockSpec((2 * n, seq), lambda b: (0, 0)),
                  pl.BlockSpec((seq, 2 * n), lambda b: (0, 0))],
        out_specs=pl.BlockSpec((seq, cw), lambda b: (b, 0)),
        compiler_params=_cparams(("parallel",)),
        name="hyena_ctx_conv",
    )(z, xg, taps, asum, bias, ff, fh, fi)


def _hyena_long(v0, x1, x2, taps, asum, hy_bias, tabs, *, bsz, seq):
    f1_full, f1_half, tmat, umat, fi = tabs
    n2 = DFT_N2
    n1 = 2 * seq // n2
    cw = BRANCH_W
    tn = 4096
    at = _lmm(f1_full, taps.reshape(1, n1, n2 * 2 * cw), tn=tn, out_dtype=BF16, name="taps_dft1")
    gspec = _mid_spectrum(at.reshape(1, 2, n1, n2, 2 * cw), tmat, asum)

    def conv(z, xg, order):
        z2 = z.reshape(bsz, n1 // 2, n2 * cw)
        a = _lmm(f1_half, z2, tn=tn, out_dtype=BF16, name="hyena_dft1")
        bk = _mid_conv(a.reshape(bsz, 2, n1, n2, cw), tmat, umat, gspec, order)
        bias_t = jnp.tile(hy_bias[order].reshape(1, cw), (1, tn // cw))
        y = _lmm(fi, bk.reshape(bsz, 2 * n1, n2 * cw), tn=tn, out_dtype=BF16,
                 gate_args=(xg.reshape(bsz, n1 // 2, n2 * cw), z2, bias_t), name="hyena_idft2")
        return y.reshape(bsz * seq, cw)

    return conv(conv(v0, x1, 0), x2, 1)


def _s5_kernel(u_ref, bre_ref, bim_ref, cre_ref, cim_ref, are_ref, aim_ref, s0_ref,
               y_ref, sfin_ref,
               uf, lhs, bur, bui, ybuf, pw_r, pw_i, car_r, car_i, st_r, st_i, *, tseg, rev):
    i = pl.program_id(1)
    sw = S5_SW
    gw = sw // S5_SUPER
    a_re = are_ref[...]
    a_im = aim_ref[...]

    @pl.when(jnp.logical_and(pl.program_id(0) == 0, i == 0))
    def _():
        def pbody(r, carry):
            cr, ci = carry
            pw_r[pl.ds(r, 1), :] = cr
            pw_i[pl.ds(r, 1), :] = ci
            return cr * a_re - ci * a_im, cr * a_im + ci * a_re
        lax.fori_loop(0, tseg, pbody, (a_re, a_im))

    @pl.when(i == 0)
    def _():
        st_r[...] = s0_ref[:, 0:sw]
        st_i[...] = s0_ref[:, sw:]

    for sg in range(S5_SUPER):
        uf[sg] = u_ref[:, sg * 128:(sg + 1) * 128].astype(F32)

    def gather(r, _):
        for sg in range(S5_SUPER):
            lhs[sg, pl.ds(pl.multiple_of(r * 8, 8), 8), :] = uf[sg, pl.ds(r, 8, stride=tseg), :]
        return 0
    lax.fori_loop(0, tseg, gather, 0)

    for sg in range(S5_SUPER):
        lb = lhs[sg].astype(BF16)
        bur[:, sg * gw:(sg + 1) * gw] = jnp.dot(lb, bre_ref[sg], preferred_element_type=F32)
        bui[:, sg * gw:(sg + 1) * gw] = jnp.dot(lb, bim_ref[sg], preferred_element_type=F32)

    for sg in range(S5_SUPER):
        cols = slice(sg * gw, (sg + 1) * gw)
        ar = jnp.broadcast_to(a_re[:, cols], (8, gw))
        ai = jnp.broadcast_to(a_im[:, cols], (8, gw))

        def sbody(k, carry, cols=cols, ar=ar, ai=ai):
            sr, si = carry
            r = (tseg - 1 - k) if rev else k
            rows = pl.ds(pl.multiple_of(r * 8, 8), 8)
            nr = ar * sr - ai * si + bur[rows, cols]
            ni = ar * si + ai * sr + bui[rows, cols]
            bur[rows, cols] = nr
            bui[rows, cols] = ni
            return nr, ni
        zero = jnp.zeros((8, gw), F32)
        lax.fori_loop(0, tseg, sbody, (zero, zero))

    at_r = pw_r[tseg - 1:tseg, :]
    at_i = pw_i[tseg - 1:tseg, :]
    end_row = 0 if rev else (tseg - 1) * 8
    cr = st_r[...]
    ci = st_i[...]
    order = range(7, -1, -1) if rev else range(8)
    for s in order:
        car_r[s:s + 1, :] = cr
        car_i[s:s + 1, :] = ci
        er = bur[end_row + s:end_row + s + 1, :]
        ei = bui[end_row + s:end_row + s + 1, :]
        cr, ci = er + at_r * cr - at_i * ci, ei + at_r * ci + at_i * cr
    st_r[...] = cr
    st_i[...] = ci
    sfin_ref[:, 0:sw] = cr
    sfin_ref[:, sw:] = ci

    for sg in range(S5_SUPER):
        cols = slice(sg * gw, (sg + 1) * gw)
        kr = car_r[:, cols]
        kim = car_i[:, cols]

        def fbody(r, _, cols=cols, kr=kr, kim=kim):
            pidx = (tseg - 1 - r) if rev else r
            pr = pw_r[pl.ds(pidx, 1), cols]
            pi_ = pw_i[pl.ds(pidx, 1), cols]
            rows = pl.ds(pl.multiple_of(r * 8, 8), 8)
            bur[rows, cols] = bur[rows, cols] + (pr * kr - pi_ * kim)
            bui[rows, cols] = bui[rows, cols] + (pr * kim + pi_ * kr)
            return 0
        lax.fori_loop(0, tseg, fbody, 0)

    for sg in range(S5_SUPER):
        cols = slice(sg * gw, (sg + 1) * gw)
        ybuf[sg] = (jnp.dot(bur[:, cols].astype(BF16), cre_ref[sg], preferred_element_type=F32)
                    + jnp.dot(bui[:, cols].astype(BF16), cim_ref[sg], preferred_element_type=F32))

    def scatter(r, _):
        for sg in range(S5_SUPER):
            y_ref[sg, pl.ds(r, 8, stride=tseg), :] = ybuf[sg, pl.ds(pl.multiple_of(r * 8, 8), 8), :]
        return 0
    lax.fori_loop(0, tseg, scatter, 0)


def _s5_scan(proj, s5p, s0, *, bsz, seq, tseg, rev):
    bre, bim, cre, cim, are, aim = s5p
    tr = 8 * tseg
    nt = seq // tr
    sw = S5_SW
    gw = sw // S5_SUPER
    tile = (lambda b, i: (b * nt + (nt - 1 - i), U_U)) if rev else (lambda b, i: (b * nt + i, U_U))
    otile = (lambda b, i: (0, b * nt + (nt - 1 - i), 0)) if rev else (lambda b, i: (0, b * nt + i, 0))
    full = lambda shape: pl.BlockSpec(shape, lambda b, i: tuple(0 for _ in shape))
    return pl.pallas_call(
        functools.partial(_s5_kernel, tseg=tseg, rev=rev),
        out_shape=(jax.ShapeDtypeStruct((S5_SUPER, bsz * seq, 128), F32),
                   jax.ShapeDtypeStruct((bsz, 1, 2 * sw), F32)),
        grid=(bsz, nt),
        in_specs=[pl.BlockSpec((tr, UNIT), tile),
                  full((S5_SUPER, 128, gw)), full((S5_SUPER, 128, gw)),
                  full((S5_SUPER, gw, 128)), full((S5_SUPER, gw, 128)),
                  full((1, sw)), full((1, sw)),
                  pl.BlockSpec((None, 1, 2 * sw), lambda b, i: (b, 0, 0))],
        out_specs=(pl.BlockSpec((S5_SUPER, tr, 128), otile),
                   pl.BlockSpec((None, 1, 2 * sw), lambda b, i: (b, 0, 0))),
        scratch_shapes=[pltpu.VMEM((S5_SUPER, tr, 128), F32), pltpu.VMEM((S5_SUPER, tr, 128), F32),
                        pltpu.VMEM((tr, sw), F32), pltpu.VMEM((tr, sw), F32),
                        pltpu.VMEM((S5_SUPER, tr, 128), F32),
                        pltpu.VMEM((tseg, sw), F32), pltpu.VMEM((tseg, sw), F32),
                        pltpu.VMEM((8, sw), F32), pltpu.VMEM((8, sw), F32),
                        pltpu.VMEM((1, sw), F32), pltpu.VMEM((1, sw), F32)],
        compiler_params=_cparams(("arbitrary", "arbitrary")),
        name="s5_bwd" if rev else "s5_fwd",
    )(proj, bre, bim, cre, cim, are, aim, s0)


def _s5_params(a_re, a_im, log_step, b_re, b_im, c_re, c_im):
    g, p, ci = S5_GROUPS, S5_STATE, S5_GROUP
    dt = jnp.exp(log_step)[:, None]
    mag = jnp.exp(a_re * dt)
    ar, ai = mag * jnp.cos(a_im * dt), mag * jnp.sin(a_im * dt)
    den = a_re * a_re + a_im * a_im
    fr = ((ar - 1.0) * a_re + ai * a_im) / den
    fi = (ai * a_re - (ar - 1.0) * a_im) / den
    bbr = fr[..., None] * b_re - fi[..., None] * b_im
    bbi = fr[..., None] * b_im + fi[..., None] * b_re
    eye = jnp.eye(8, dtype=F32)

    def blockdiag_b(m):
        m4 = m.reshape(S5_SUPER, 8, p, ci)
        return jnp.einsum('sgpc,gh->sgchp', m4, eye).reshape(S5_SUPER, 8 * ci, 8 * p).astype(BF16)

    def blockdiag_c(m):
        m4 = m.reshape(S5_SUPER, 8, ci, p)
        return jnp.einsum('sgcp,gh->sgphc', m4, eye).reshape(S5_SUPER, 8 * p, 8 * ci).astype(BF16)

    return (blockdiag_b(bbr), blockdiag_b(bbi), blockdiag_c(c_re), blockdiag_c(-c_im),
            ar.reshape(1, g * p), ai.reshape(1, g * p))


def _merge_kernel(ya_ref, yh_ref, hg_ref, yatt_ref, attg_ref, ysf_ref, ysb_ref, u_ref, s5g_ref,
                  mg0_ref, mg1_ref, mg2_ref, mg3_ref, x_ref, mod_ref, s5d_ref, wglu_ref, wb_ref, wout_ref,
                  lng_ref, lnb_ref, o_ref, *, alpha):
    f = lambda r: r[...].astype(F32)
    y_a = f(ya_ref)
    y_h = f(yh_ref) * _silu(f(hg_ref))
    y_c = f(yatt_ref) * _silu(f(attg_ref))
    ys = jnp.concatenate([ysf_ref[sg] + ysb_ref[sg] for sg in range(S5_SUPER)], axis=1)
    y = ys + s5d_ref[...] * f(u_ref)
    zg = jax.nn.gelu(y)
    glu = jnp.dot(zg.astype(BF16), wglu_ref[...], preferred_element_type=F32)
    y_d = zg * jax.nn.sigmoid(glu) * _silu(f(s5g_ref))
    mix = None
    for n, (yn, mg) in enumerate(((y_a, mg0_ref), (y_h, mg1_ref), (y_c, mg2_ref), (y_d, mg3_ref))):
        term = jax.nn.sigmoid(f(mg)) * jnp.dot(yn.astype(BF16), wb_ref[n], preferred_element_type=F32)
        mix = term if mix is None else mix + term
    out = jnp.dot(mix.astype(BF16), wout_ref[...], preferred_element_type=F32)
    gate = mod_ref[2:3, :]
    r = alpha * x_ref[...] + gate * out
    mu = jnp.mean(r, axis=1, keepdims=True)
    rc = r - mu
    var = jnp.mean(rc * rc, axis=1, keepdims=True)
    o_ref[...] = rc * lax.rsqrt(var + LN_EPS) * lng_ref[...] + lnb_ref[...]


def _merge(ya, yh, yatt, ysf, ysb, proj, x2d, mod3, s5d, wglu, wb, wout, lng, lnb, *, seq, tm, alpha):
    m = x2d.shape[0]
    nt_seq = seq // tm
    nb = mod3.shape[0]
    mod_map = (lambda i: (i // nt_seq, 0, 0)) if nb > 1 else (lambda i: (0, 0, 0))
    row = lambda w_: pl.BlockSpec((tm, w_), lambda i: (i, 0))
    pcol = lambda unit: pl.BlockSpec((tm, UNIT), lambda i: (i, unit))
    mcol = lambda n: pl.BlockSpec((tm, 2 * UNIT), lambda i: (i, U_MERGE // 2 + n))
    full = lambda shape: pl.BlockSpec(shape, lambda i: tuple(0 for _ in shape))
    w_ = BRANCH_W
    s5row = pl.BlockSpec((S5_SUPER, tm, 128), lambda i: (0, i, 0))
    return pl.pallas_call(
        functools.partial(_merge_kernel, alpha=alpha),
        out_shape=jax.ShapeDtypeStruct((m, D_MODEL), F32),
        grid=(m // tm,),
        in_specs=[row(w_), row(w_), pcol(U_HG), row(w_), pcol(U_ATTG), s5row, s5row, pcol(U_U), pcol(U_S5G),
                  mcol(0), mcol(1), mcol(2), mcol(3), row(D_MODEL),
                  pl.BlockSpec((None, 3, D_MODEL), mod_map),
                  full((1, w_)), full((w_, w_)), full((N_BRANCH, w_, D_MODEL)), full((D_MODEL, D_MODEL)),
                  full((1, D_MODEL)), full((1, D_MODEL))],
        out_specs=row(D_MODEL),
        compiler_params=_cparams(("parallel",)),
        name="merge_out_norm",
    )(ya, yh, proj, yatt, proj, ysf, ysb, proj, proj, proj, proj, proj, proj, x2d, mod3,
      s5d, wglu, wb, wout, lng, lnb)


def _rope_tables(n_lat):
    rows = n_lat // GRID_W
    row = jnp.broadcast_to(jnp.arange(rows)[:, None], (rows, GRID_W)).reshape(-1)
    col = jnp.broadcast_to(jnp.arange(GRID_W)[None, :], (rows, GRID_W)).reshape(-1)
    half = HEAD_DIM // 2
    inv = 1.0 / (ROPE_BASE ** (jnp.arange(0, half, 2, dtype=F32) / half))
    ar, ac = row[:, None] * inv, col[:, None] * inv
    cos64 = jnp.concatenate([jnp.cos(ar), jnp.cos(ar), jnp.cos(ac), jnp.cos(ac)], axis=-1)
    sin64 = jnp.concatenate([-jnp.sin(ar), jnp.sin(ar), -jnp.sin(ac), jnp.sin(ac)], axis=-1)
    return jnp.tile(cos64, (1, 2)), jnp.tile(sin64, (1, 2))


def _pick(n, prefs):
    for t in prefs:
        if n % t == 0:
            return t
    return n


def kernel(x, c, ctx, c_ctx, w_mod, b_mod, w_in, conv_a, conv_h, hy_w1, hy_b1, hy_w2, hy_b2, hy_w3, hy_freq,
           hy_delta, hy_bias, lam_q1, lam_k1, lam_q2, lam_k2, attn_norm_g, s5_a_re, s5_a_im, s5_log_step,
           s5_b_re, s5_b_im, s5_c_re, s5_c_im, s5_d, s5_w_glu, w_branch, w_out, ln_g, ln_b):
    bsz, seq, d = x.shape
    seq_c = ctx.shape[1]
    depth = w_in.shape[0]
    alpha = (2.0 * depth) ** 0.25
    assert d == D_MODEL and seq % 1024 == 0 and seq_c % 256 == 0 and bsz + 1 <= 8

    cos_t, sin_t = _rope_tables(seq)
    dft_lat = _dft_tables(seq)
    dft_ctx = _dft_tables_small(seq_c)
    cvec = jnp.zeros((8, d), F32).at[0:bsz].set(c).at[bsz].set(c_ctx)
    tm_in = _pick(seq, (2048, 1024))
    tm_el = _pick(seq, (512,))
    tq = _pick(seq, (256,))
    ck = _pick(seq, (1024, 512))
    tseg_lat = 128
    tseg_ctx = seq_c // 8

    x2 = x.reshape(bsz * seq, d)
    xc2 = ctx.reshape(bsz * seq_c, d)
    for l in range(depth):
        last = l == depth - 1
        lam_init = 0.8 - 0.6 * math.exp(-0.3 * l)
        mod = _modulation(cvec, w_mod[l], b_mod[l]).reshape(8, 3, d)
        mod_lat, mod_ctx = mod[0:bsz], mod[bsz:bsz + 1]
        w_l = w_in[l].reshape(d, N_UNITS, UNIT)[:, jnp.array(UNIT_PERM)].reshape(d, PROJ_W).astype(BF16)
        lamp = jnp.stack([lam_q1[l], lam_k1[l], lam_q2[l], lam_k2[l]], axis=0)
        g_att = attn_norm_g[l].reshape(1, V_DIM)
        s5f = _s5_params(s5_a_re[l, 0], s5_a_im[l, 0], s5_log_step[l, 0], s5_b_re[l, 0], s5_b_im[l, 0],
                         s5_c_re[l, 0], s5_c_im[l, 0])
        s5b = _s5_params(s5_a_re[l, 1], s5_a_im[l, 1], s5_log_step[l, 1], s5_b_re[l, 1], s5_b_im[l, 1],
                         s5_c_re[l, 1], s5_c_im[l, 1])
        hp = (hy_w1[l], hy_b1[l], hy_w2[l], hy_b2[l], hy_w3[l], hy_freq[l], hy_delta[l])
        wglu = s5_w_glu[l].astype(BF16)
        wb = w_branch[l].astype(BF16)
        wout = w_out[l].astype(BF16)
        s5d = s5_d[l].reshape(1, BRANCH_W)
        lng, lnb = ln_g[l].reshape(1, d), ln_b[l].reshape(1, d)

        ncols_c = 3 * 1024 if last else PROJ_W
        projc = _inproj(xc2, mod_ctx, w_l[:, :ncols_c], cos_t, sin_t, seq=seq_c, rope=False,
                        tm=bsz * seq_c, ncols=ncols_c)
        zero_state = jnp.zeros((bsz, 1, 2 * S5_SW), F32)
        ycf, scf = _s5_scan(projc, s5f, zero_state, bsz=bsz, seq=seq_c, tseg=tseg_ctx, rev=False)
        ycb, scb = _s5_scan(projc, s5b, zero_state, bsz=bsz, seq=seq_c, tseg=tseg_ctx, rev=True)

        proj = _inproj(x2, mod_lat, w_l, cos_t, sin_t, seq=seq, rope=True, tm=tm_in, ncols=PROJ_W)
        yatt = _attention_lat(proj, projc, lamp, g_att, bsz=bsz, seq=seq, seq_c=seq_c,
                              lam_init=lam_init, tq=tq, nsub=2, ck=ck)
        ysf, _ = _s5_scan(proj, s5f, scf, bsz=bsz, seq=seq, tseg=tseg_lat, rev=False)
        ysb, _ = _s5_scan(proj, s5b, scb, bsz=bsz, seq=seq, tseg=tseg_lat, rev=True)
        ya, v0, x1, x2h = _convgate(proj, conv_a[l], conv_h[l], bsz=bsz, seq=seq, tm=tm_el)
        taps, asum = _hyena_taps(hp, seq=seq, tr=512)
        yh = _hyena_long(v0, x1, x2h, taps, asum, hy_bias[l], dft_lat, bsz=bsz, seq=seq)
        x_new = _merge(ya, yh, yatt, ysf, ysb, proj, x2, mod_lat, s5d, wglu, wb, wout, lng, lnb,
                       seq=seq, tm=tm_el, alpha=alpha)

        if not last:
            yatt_c = _attention(projc, None, projc, lamp, g_att, bsz=bsz, seq_q=seq_c, seq_kv=0, seq_c=seq_c,
                                lam_init=lam_init, tq=seq_c, tk=seq_c)
            ya_c, v0c, x1c, x2c = _convgate(projc, conv_a[l], conv_h[l], bsz=bsz, seq=seq_c, tm=seq_c)
            taps_c, asum_c = _hyena_taps(hp, seq=seq_c, tr=seq_c)
            b0 = hy_bias[l, 0].reshape(1, BRANCH_W)
            b1 = hy_bias[l, 1].reshape(1, BRANCH_W)
            z1c = _ctx_longconv(v0c, x1c, taps_c, asum_c, b0, dft_ctx, 0, bsz=bsz, seq=seq_c)
            yh_c = _ctx_longconv(z1c, x2c, taps_c, asum_c, b1, dft_ctx, 1, bsz=bsz, seq=seq_c)
            xc2 = _merge(ya_c, yh_c, yatt_c, ycf, ycb, projc, xc2, mod_ctx, s5d, wglu, wb, wout, lng, lnb,
                         seq=seq_c, tm=seq_c, alpha=alpha)
        x2 = x_new
    return x2.reshape(bsz, seq, d)
```

```python
import functools
import math

import jax
import jax.numpy as jnp
from jax import lax
from jax.experimental import pallas as pl
from jax.experimental.pallas import tpu as pltpu

F32 = jnp.float32
BF16 = jnp.bfloat16
HIGHEST = lax.Precision.HIGHEST

D_MODEL = 1024
BRANCH_W = 512
N_HEADS = 4
HEAD_DIM = 64
V_DIM = 128
GRID_W = 64
ROPE_BASE = 10000.0
HYENA_BANDS = 16
HYENA_HIDDEN = 64
HYENA_SHIFT = 0.05
S5_GROUP = 16
S5_GROUPS = 32
S5_STATE = 64
S5_SUPER = 4
S5_SW = S5_GROUPS * S5_STATE
LN_EPS = 1e-5
N_BRANCH = 4
UNIT = 512
N_UNITS = 22
PROJ_W = N_UNITS * UNIT
UNIT_PERM = (0, 1, 11, 12, 2, 4, 5, 7, 8, 9, 3, 6, 13, 10, 14, 15, 16, 17, 18, 19, 20, 21)
U_K, U_V, U_Q, U_ATTG, U_U, U_CONV, U_AB, U_S5G, U_HG, U_MERGE = 0, 1, 2, 3, 4, 5, 10, 12, 13, 14
DFT_N2 = 128
MID_K1_PER_STEP = 4
VMEM_LIMIT = 56 * 1024 * 1024


def _cparams(sem):
    return pltpu.CompilerParams(dimension_semantics=sem, vmem_limit_bytes=VMEM_LIMIT)


def _sigmoid(v):
    return 0.5 * jnp.tanh(0.5 * v) + 0.5


def _silu(v):
    return v * _sigmoid(v)


def _mod_kernel(s_ref, w_ref, b_ref, o_ref):
    s = _silu(s_ref[...])
    o_ref[...] = jnp.dot(s, w_ref[...], preferred_element_type=F32, precision=HIGHEST) + b_ref[...]


def _modulation(cvec, w_mod, b_mod, layer):
    depth, _, n = w_mod.shape
    tn = 512
    return pl.pallas_call(
        _mod_kernel,
        out_shape=jax.ShapeDtypeStruct((8, n), F32),
        grid=(n // tn,),
        in_specs=[pl.BlockSpec((8, D_MODEL), lambda j: (0, 0)),
                  pl.BlockSpec((None, D_MODEL, tn), lambda j: (layer, 0, j)),
                  pl.BlockSpec((None, 1, tn), lambda j: (layer, 0, j))],
        out_specs=pl.BlockSpec((8, tn), lambda j: (0, j)),
        compiler_params=_cparams(("parallel",)),
        name="modulation",
    )(cvec, w_mod, b_mod.reshape(depth, 1, n))


def _inproj_kernel(x_ref, mod_ref, w_ref, cos_ref, sin_ref, o_ref, h_ref, *, rope):
    j = pl.program_id(1)

    @pl.when(j == 0)
    def _():
        shift = mod_ref[0:1, :]
        scale = mod_ref[1:2, :]
        h_ref[...] = (x_ref[...] * (1.0 + scale) + shift).astype(BF16)

    def project():
        return jnp.dot(h_ref[...], w_ref[...], preferred_element_type=F32)

    if not rope:
        o_ref[...] = project().astype(o_ref.dtype)
        return

    @pl.when(j < 2)
    def _():
        acc = project()
        cs = cos_ref[...]
        sn = sin_ref[...]
        lane = lax.broadcasted_iota(jnp.int32, cs.shape, 1)
        first = (lane % 32) < 16
        for cb in range(UNIT // 128):
            t = acc[:, cb * 128:(cb + 1) * 128]
            partner = jnp.where(first, pltpu.roll(t, 128 - 16, 1), pltpu.roll(t, 16, 1))
            o_ref[:, cb * 128:(cb + 1) * 128] = (t * cs + partner * sn).astype(o_ref.dtype)
        o_ref[:, UNIT:] = acc[:, UNIT:].astype(o_ref.dtype)

    @pl.when(j >= 2)
    def _():
        o_ref[...] = project().astype(o_ref.dtype)


def _inproj(x2d, mod3, w, cos_t, sin_t, *, seq, rope, tm, ncols):
    m = x2d.shape[0]
    tn = 1024
    nt_seq = max(seq // tm, 1)
    nb = mod3.shape[0]
    mod_map = (lambda i, j: (i // nt_seq, 0, 0)) if nb > 1 else (lambda i, j: (0, 0, 0))
    return pl.pallas_call(
        functools.partial(_inproj_kernel, rope=rope),
        out_shape=jax.ShapeDtypeStruct((m, ncols), BF16),
        grid=(m // tm, ncols // tn),
        in_specs=[pl.BlockSpec((tm, D_MODEL), lambda i, j: (i, 0)),
                  pl.BlockSpec((None, 3, D_MODEL), mod_map),
                  pl.BlockSpec((D_MODEL, tn), lambda i, j: (0, j)),
                  pl.BlockSpec((tm, 128), lambda i, j: (i % nt_seq, 0)),
                  pl.BlockSpec((tm, 128), lambda i, j: (i % nt_seq, 0))],
        out_specs=pl.BlockSpec((tm, tn), lambda i, j: (i, j)),
        scratch_shapes=[pltpu.VMEM((tm, D_MODEL), BF16)],
        compiler_params=_cparams(("parallel", "arbitrary")),
        name="inproj_rope" if rope else "inproj_ctx",
    )(x2d, mod3, w, cos_t, sin_t)


def _attn_kernel(*refs, n_lat, tq, lam_init):
    if n_lat:
        q_ref, k_ref, v_ref, kc_ref, vc_ref, lamp_ref, g_ref, o_ref, qz, m_s, l_s, acc = refs
    else:
        q_ref, kc_ref, vc_ref, lamp_ref, g_ref, o_ref, qz, m_s, l_s, acc = refs
    ki = pl.program_id(3)

    @pl.when(ki == 0)
    def _():
        q = q_ref[...].astype(F32) * (HEAD_DIM ** -0.5)
        lane = lax.broadcasted_iota(jnp.int32, q.shape, 1)
        qz[0:tq, :] = jnp.where(lane < HEAD_DIM, q, 0.0).astype(BF16)
        qz[tq:, :] = jnp.where(lane >= HEAD_DIM, q, 0.0).astype(BF16)
        m_s[...] = jnp.full(m_s.shape, -jnp.inf, F32)
        l_s[...] = jnp.zeros(l_s.shape, F32)
        acc[...] = jnp.zeros(acc.shape, F32)

    def update(k, v):
        s = lax.dot_general(qz[...], k, (((1,), (1,)), ((), ())), preferred_element_type=F32)
        m_prev = m_s[...]
        m_new = jnp.maximum(m_prev, jnp.max(s, axis=1, keepdims=True))
        alpha = jnp.exp(m_prev - m_new)
        p = jnp.exp(s - m_new)
        l_s[...] = alpha * l_s[...] + jnp.sum(p, axis=1, keepdims=True)
        acc[...] = alpha * acc[...] + jnp.dot(p.astype(BF16), v, preferred_element_type=F32)
        m_s[...] = m_new

    if n_lat:
        @pl.when(ki < n_lat)
        def _():
            update(k_ref[...], v_ref[...])

        @pl.when(ki == n_lat)
        def _():
            update(kc_ref[...], vc_ref[...])
    else:
        update(kc_ref[...], vc_ref[...])

    @pl.when(ki == n_lat)
    def _():
        o = acc[...] / l_s[...]
        lp = lamp_ref[...]
        lam = (jnp.exp(jnp.sum(lp[0:1] * lp[1:2], axis=1, keepdims=True))
               - jnp.exp(jnp.sum(lp[2:3] * lp[3:4], axis=1, keepdims=True)) + lam_init)
        od = o[0:tq] - lam * o[tq:]
        od = od * lax.rsqrt(jnp.mean(od * od, axis=1, keepdims=True) + 1e-5) * g_ref[...] * (1.0 - lam_init)
        o_ref[...] = od.astype(o_ref.dtype)


def _attention(proj_q, proj_kv, proj_c, lamp, g, *, bsz, seq_q, seq_kv, seq_c, lam_init, tq, tk):
    nq = seq_q // tq
    n_lat = 0 if proj_kv is None else seq_kv // tk
    hq = U_Q * UNIT // 128
    hk = U_K * UNIT // 128
    hv = U_V * UNIT // 128
    in_specs = [pl.BlockSpec((tq, 128), lambda b, h, qi, ki: (b * nq + qi, hq + h))]
    args = [proj_q]
    if n_lat:
        kmap = lambda b, h, qi, ki: (b * n_lat + jnp.minimum(ki, n_lat - 1), hk + h)
        vmap_ = lambda b, h, qi, ki: (b * n_lat + jnp.minimum(ki, n_lat - 1), hv + h)
        in_specs += [pl.BlockSpec((tk, 128), kmap), pl.BlockSpec((tk, 128), vmap_)]
        args += [proj_kv, proj_kv]
    in_specs += [pl.BlockSpec((seq_c, 128), lambda b, h, qi, ki: (b, hk + h)),
                 pl.BlockSpec((seq_c, 128), lambda b, h, qi, ki: (b, hv + h)),
                 pl.BlockSpec((4, HEAD_DIM), lambda b, h, qi, ki: (0, 0)),
                 pl.BlockSpec((1, V_DIM), lambda b, h, qi, ki: (0, 0))]
    args += [proj_c, proj_c, lamp, g]
    return pl.pallas_call(
        functools.partial(_attn_kernel, n_lat=n_lat, tq=tq, lam_init=lam_init),
        out_shape=jax.ShapeDtypeStruct((bsz * seq_q, N_HEADS * V_DIM), BF16),
        grid=(bsz, N_HEADS, nq, n_lat + 1),
        in_specs=in_specs,
        out_specs=pl.BlockSpec((tq, V_DIM), lambda b, h, qi, ki: (b * nq + qi, h)),
        scratch_shapes=[pltpu.VMEM((2 * tq, 128), BF16), pltpu.VMEM((2 * tq, 1), F32),
                        pltpu.VMEM((2 * tq, 1), F32), pltpu.VMEM((2 * tq, V_DIM), F32)],
        compiler_params=_cparams(("parallel", "parallel", "parallel", "arbitrary")),
        name="diff_attn" if n_lat else "diff_attn_ctx",
    )(*args)


def _attn_lat_kernel(q_ref, k_ref, v_ref, kc_ref, vc_ref, lamp_ref, g_ref, o_ref,
                     qz, vext, vcext, sbuf, pbuf, abuf, m_s, acc, *, tq, nsub, ck, n_chunks, lam_init):
    @pl.when(pl.program_id(2) == 0)
    def _():
        vext[:, 0:V_DIM] = v_ref[...]
        vext[:, V_DIM:] = jnp.ones((vext.shape[0], V_DIM), BF16)
        vcext[:, 0:V_DIM] = vc_ref[...]
        vcext[:, V_DIM:] = jnp.ones((vcext.shape[0], V_DIM), BF16)

    subs = range(nsub)
    for h in subs:
        q = q_ref[h * tq:(h + 1) * tq, :].astype(F32) * (HEAD_DIM ** -0.5 * math.log2(math.e))
        lane = lax.broadcasted_iota(jnp.int32, q.shape, 1)
        qz[h, 0:tq, :] = jnp.where(lane < HEAD_DIM, q, 0.0).astype(BF16)
        qz[h, tq:, :] = jnp.where(lane >= HEAD_DIM, q, 0.0).astype(BF16)

    def qk(h, kblk):
        return lax.dot_general(qz[h], kblk, (((1,), (1,)), ((), ())), preferred_element_type=F32)

    def kchunk(c):
        return k_ref[pl.ds(pl.multiple_of(c * ck, ck), ck), :]

    def vchunk(c):
        return vext[pl.ds(pl.multiple_of(c * ck, ck), ck), :]

    def score(slot, c):
        kblk = kchunk(c)
        for h in subs:
            sbuf[h, slot] = qk(h, kblk)

    def softmax(slot):
        for h in subs:
            s = sbuf[h, slot]
            m_prev = m_s[h]
            m_new = jnp.maximum(m_prev, jnp.max(s, axis=1, keepdims=True))
            abuf[h, slot] = jnp.exp2(m_prev - m_new)
            pbuf[h, slot] = jnp.exp2(s - m_new).astype(BF16)
            m_s[h] = m_new

    def pv(slot, c):
        vblk = vchunk(c)
        for h in subs:
            acc[h] = abuf[h, slot] * acc[h] + jnp.dot(pbuf[h, slot], vblk, preferred_element_type=F32)

    for h in subs:
        s = qk(h, kc_ref[...])
        m0 = jnp.max(s, axis=1, keepdims=True)
        m_s[h] = m0
        acc[h] = jnp.dot(jnp.exp2(s - m0).astype(BF16), vcext[...], preferred_element_type=F32)

    score(0, 0)
    softmax(0)
    score(1, 1)

    def body(j, _):
        a = 2 * j
        pv(0, a)
        score(0, a + 2)
        softmax(1)
        pv(1, a + 1)
        score(1, a + 3)
        softmax(0)
        return 0
    lax.fori_loop(0, n_chunks // 2 - 1, body, 0)

    pv(0, n_chunks - 2)
    softmax(1)
    pv(1, n_chunks - 1)

    lp = lamp_ref[...]
    lam = (jnp.exp(jnp.sum(lp[0:1] * lp[1:2], axis=1, keepdims=True))
           - jnp.exp(jnp.sum(lp[2:3] * lp[3:4], axis=1, keepdims=True)) + lam_init)
    for h in subs:
        a_ = acc[h]
        o = a_[:, 0:V_DIM] / a_[:, V_DIM:]
        od = o[0:tq] - lam * o[tq:]
        od = od * lax.rsqrt(jnp.mean(od * od, axis=1, keepdims=True) + 1e-5) * g_ref[...] * (1.0 - lam_init)
        o_ref[h * tq:(h + 1) * tq, :] = od.astype(o_ref.dtype)


def _attention_lat(proj, proj_c, lamp, g, *, bsz, seq, seq_c, lam_init, tq, nsub, ck):
    tqs = tq * nsub
    nq = seq // tqs
    n_chunks = seq // ck
    assert n_chunks % 2 == 0 and n_chunks >= 2
    hq = U_Q * UNIT // 128
    hk = U_K * UNIT // 128
    hv = U_V * UNIT // 128
    return pl.pallas_call(
        functools.partial(_attn_lat_kernel, tq=tq, nsub=nsub, ck=ck, n_chunks=n_chunks, lam_init=lam_init),
        out_shape=jax.ShapeDtypeStruct((bsz * seq, N_HEADS * V_DIM), BF16),
        grid=(bsz, N_HEADS, nq),
        in_specs=[pl.BlockSpec((tqs, 128), lambda b, h, qi: (b * nq + qi, hq + h)),
                  pl.BlockSpec((seq, 128), lambda b, h, qi: (b, hk + h)),
                  pl.BlockSpec((seq, 128), lambda b, h, qi: (b, hv + h)),
                  pl.BlockSpec((seq_c, 128), lambda b, h, qi: (b, hk + h)),
                  pl.BlockSpec((seq_c, 128), lambda b, h, qi: (b, hv + h)),
                  pl.BlockSpec((4, HEAD_DIM), lambda b, h, qi: (0, 0)),
                  pl.BlockSpec((1, V_DIM), lambda b, h, qi: (0, 0))],
        out_specs=pl.BlockSpec((tqs, V_DIM), lambda b, h, qi: (b * nq + qi, h)),
        scratch_shapes=[pltpu.VMEM((nsub, 2 * tq, 128), BF16),
                        pltpu.VMEM((seq, 2 * V_DIM), BF16), pltpu.VMEM((seq_c, 2 * V_DIM), BF16),
                        pltpu.VMEM((nsub, 2, 2 * tq, ck), F32), pltpu.VMEM((nsub, 2, 2 * tq, ck), BF16),
                        pltpu.VMEM((nsub, 2, 2 * tq, 1), F32), pltpu.VMEM((nsub, 2 * tq, 1), F32),
                        pltpu.VMEM((nsub, 2 * tq, 2 * V_DIM), F32)],
        compiler_params=_cparams(("parallel", "parallel", "arbitrary")),
        name="diff_attn",
    )(proj, proj, proj, proj_c, proj_c, lamp, g)


def _conv3(cur, prv, nxt, w):
    tm = cur.shape[0]
    row = lax.broadcasted_iota(jnp.int32, cur.shape, 0)
    dn = jnp.where(row == 0, prv, pltpu.roll(cur, 1, 0))
    up = jnp.where(row == tm - 1, nxt, pltpu.roll(cur, tm - 1, 0))
    return dn * w[0:1] + cur * w[1:2] + up * w[2:3]


def _convgate_kernel(main_ref, prev_ref, next_ref, ab_ref, ca_ref, ch_ref,
                     ya_ref, v0_ref, x1_ref, x2_ref, *, nt):
    i = pl.program_id(1)
    w_ = BRANCH_W
    m = main_ref[...].astype(F32)
    pv = jnp.where(i > 0, prev_ref[15:16, :].astype(F32), 0.0)
    nx = jnp.where(i < nt - 1, next_ref[0:1, :].astype(F32), 0.0)
    ab = ab_ref[...].astype(F32)
    p = m[:, 0:w_] * m[:, w_:2 * w_]
    p_prev = pv[:, 0:w_] * pv[:, w_:2 * w_]
    p_next = nx[:, 0:w_] * nx[:, w_:2 * w_]
    ya = ab[:, 0:w_] * _conv3(p, p_prev, p_next, ca_ref[...]) * _silu(ab[:, w_:])
    ya_ref[...] = ya.astype(ya_ref.dtype)
    ch = ch_ref[...]
    for n, ref in enumerate((v0_ref, x1_ref, x2_ref)):
        lo, hi = (2 + n) * w_, (3 + n) * w_
        ref[...] = _conv3(m[:, lo:hi], pv[:, lo:hi], nx[:, lo:hi], ch[:, n * w_:(n + 1) * w_]).astype(ref.dtype)


def _convgate(proj, conv_a, conv_h, *, bsz, seq, tm):
    m = bsz * seq
    nt = seq // tm
    cw = 5 * UNIT
    hb = 16
    nhb = m // hb
    out = jax.ShapeDtypeStruct((m, BRANCH_W), BF16)
    ospec = pl.BlockSpec((tm, BRANCH_W), lambda b, i: (b * nt + i, 0))
    return pl.pallas_call(
        functools.partial(_convgate_kernel, nt=nt),
        out_shape=(out, out, out, out),
        grid=(bsz, nt),
        in_specs=[pl.BlockSpec((tm, cw), lambda b, i: (b * nt + i, U_CONV * UNIT // cw)),
                  pl.BlockSpec((hb, cw), lambda b, i: (jnp.maximum((b * nt + i) * (tm // hb) - 1, 0), 1)),
                  pl.BlockSpec((hb, cw), lambda b, i: (jnp.minimum((b * nt + i + 1) * (tm // hb), nhb - 1), 1)),
                  pl.BlockSpec((tm, 2 * UNIT), lambda b, i: (b * nt + i, U_AB * UNIT // (2 * UNIT))),
                  pl.BlockSpec((3, BRANCH_W), lambda b, i: (0, 0)),
                  pl.BlockSpec((3, 3 * BRANCH_W), lambda b, i: (0, 0))],
        out_specs=(ospec, ospec, ospec, ospec),
        compiler_params=_cparams(("parallel", "parallel")),
        name="convgate",
    )(proj, proj, proj, proj, conv_a, conv_h)


def _taps_kernel(cols_ref, w1c_ref, w1s_ref, w2_ref, w3_ref, dl_ref, taps_ref, asum_ref, *, seq, tr):
    i = pl.program_id(0)

    def source(m):
        return jnp.where(m < seq, m, 2 * seq - m).astype(F32)

    src_l = source(i * tr + lax.broadcasted_iota(jnp.int32, (1, tr), 1))
    t_l = src_l / (seq - 1.0)
    w_l = (2.0 * math.pi / seq) * src_l
    band = lax.broadcasted_iota(jnp.int32, (HYENA_BANDS, 1), 0).astype(F32)
    f = 1e-4 + band * ((HYENA_BANDS - 1.0 - 1e-4) / (HYENA_BANDS - 1.0))
    ang = f * w_l
    cols = cols_ref[...]
    pre = (cols[:, 0:1] * t_l
           + jnp.dot(w1c_ref[...], jnp.cos(ang), preferred_element_type=F32, precision=HIGHEST)
           + jnp.dot(w1s_ref[...], -jnp.sin(ang), preferred_element_type=F32, precision=HIGHEST)
           + cols[:, 1:2])
    h = jnp.sin(cols[:, 3:4] * pre)
    h = jnp.sin(cols[:, 4:5] * (jnp.dot(w2_ref[...], h, preferred_element_type=F32, precision=HIGHEST)
                                + cols[:, 2:3]))
    h = lax.dot_general(h, w3_ref[...], (((0,), (0,)), ((), ())), preferred_element_type=F32,
                        precision=HIGHEST)
    mrow = i * tr + lax.broadcasted_iota(jnp.int32, (tr, 1), 0)
    t = source(mrow) / (seq - 1.0)
    decay = jnp.exp(-t * jnp.abs(dl_ref[...]))
    out = h * (decay + HYENA_SHIFT)
    out = jnp.where(mrow == seq, 0.0, out)
    taps_ref[...] = out

    @pl.when(i == 0)
    def _():
        asum_ref[...] = jnp.zeros(asum_ref.shape, F32)

    asum_ref[...] += jnp.sum(jnp.abs(out), axis=0, keepdims=True)


def _hyena_taps(hp, *, seq, tr):
    w1, b1, w2, b2, w3, freq, deltas = hp
    hh = HYENA_HIDDEN
    cw = 2 * BRANCH_W
    w3d = w3.reshape(hh, 2, 2, BRANCH_W).transpose(2, 0, 1, 3).reshape(2, hh, cw)
    dld = deltas.transpose(1, 0, 2).reshape(2, 1, cw)
    cols = jnp.stack([w1[0], b1, b2, freq[0], freq[1]], axis=1)
    nt = 2 * seq // tr
    full = lambda shape: pl.BlockSpec(shape, lambda i: tuple(0 for _ in shape))
    return pl.pallas_call(
        functools.partial(_taps_kernel, seq=seq, tr=tr),
        out_shape=(jax.ShapeDtypeStruct((2 * seq, cw), F32), jax.ShapeDtypeStruct((1, cw), F32)),
        grid=(nt,),
        in_specs=[full((hh, 5)), full((hh, HYENA_BANDS)), full((hh, HYENA_BANDS)), full((hh, hh)),
                  pl.BlockSpec((None, hh, cw), lambda i: (i // (nt // 2), 0, 0)),
                  pl.BlockSpec((None, 1, cw), lambda i: (i // (nt // 2), 0, 0))],
        out_specs=(pl.BlockSpec((tr, cw), lambda i: (i, 0)), pl.BlockSpec((1, cw), lambda i: (0, 0))),
        compiler_params=_cparams(("arbitrary",)),
        name="hyena_taps",
    )(cols, w1[1:1 + HYENA_BANDS].T, w1[1 + HYENA_BANDS:].T, w2.T, w3d, dld)


def _lmm_kernel(a_ref, x_ref, *rest, gate):
    if gate:
        xg_ref, z_ref, bias_ref, o_ref = rest
    else:
        (o_ref,) = rest
    acc = jnp.dot(a_ref[...], x_ref[...].astype(BF16), preferred_element_type=F32)
    if gate:
        acc = xg_ref[...].astype(F32) * (acc + bias_ref[...] * z_ref[...].astype(F32))
    o_ref[...] = acc.astype(o_ref.dtype)


def _lmm(a, x, *, tn, out_dtype, gate_args=None, name):
    nb, k, n = x.shape
    ma = a.shape[0]
    in_specs = [pl.BlockSpec((ma, k), lambda b, j: (0, 0)), pl.BlockSpec((None, k, tn), lambda b, j: (b, 0, j))]
    args = [a, x]
    if gate_args is not None:
        xg, z, bias_t = gate_args
        in_specs += [pl.BlockSpec((None, ma, tn), lambda b, j: (b, 0, j)),
                     pl.BlockSpec((None, ma, tn), lambda b, j: (b, 0, j)),
                     pl.BlockSpec((1, tn), lambda b, j: (0, 0))]
        args += [xg, z, bias_t]
    return pl.pallas_call(
        functools.partial(_lmm_kernel, gate=gate_args is not None),
        out_shape=jax.ShapeDtypeStruct((nb, ma, n), out_dtype),
        grid=(nb, n // tn),
        in_specs=in_specs,
        out_specs=pl.BlockSpec((None, ma, tn), lambda b, j: (b, 0, j)),
        compiler_params=_cparams(("parallel", "parallel")),
        name=name,
    )(*args)


def _mid_kernel(a_ref, t_ref, *rest, conv):
    if conv:
        g_ref, u_ref, o_ref = rest
    else:
        asum_ref, o_ref = rest
    kb, n2 = a_ref.shape[1], a_ref.shape[2]
    for kk in range(kb):
        a = jnp.concatenate([a_ref[0, kk], a_ref[1, kk]], axis=0)
        xk = jnp.dot(t_ref[kk], a, preferred_element_type=F32)
        if conv:
            xr, xi = xk[0:n2], xk[n2:]
            gr, gi = g_ref[kk, 0], g_ref[kk, 1]
            yk = jnp.concatenate([xr * gr - xi * gi, xr * gi + xi * gr], axis=0).astype(BF16)
            bk = jnp.dot(u_ref[kk], yk, preferred_element_type=F32)
            o_ref[0, kk] = bk[0:n2].astype(o_ref.dtype)
            o_ref[1, kk] = bk[n2:].astype(o_ref.dtype)
        else:
            xk = xk / (asum_ref[...] + 1e-6)
            o_ref[kk, 0] = xk[0:n2]
            o_ref[kk, 1] = xk[n2:]


def _mid_spectrum(a5, tmat, asum):
    _, _, n1, n2, cw = a5.shape
    kb = _pick(n1, (MID_K1_PER_STEP,))
    return pl.pallas_call(
        functools.partial(_mid_kernel, conv=False),
        out_shape=jax.ShapeDtypeStruct((n1, 2, n2, cw), F32),
        grid=(n1 // kb,),
        in_specs=[pl.BlockSpec((None, 2, kb, n2, cw), lambda k: (0, 0, k, 0, 0)),
                  pl.BlockSpec((kb, 2 * n2, 2 * n2), lambda k: (k, 0, 0)),
                  pl.BlockSpec((1, cw), lambda k: (0, 0))],
        out_specs=pl.BlockSpec((kb, 2, n2, cw), lambda k: (k, 0, 0, 0)),
        compiler_params=_cparams(("parallel",)),
        name="hyena_spectrum",
    )(a5, tmat, asum)


def _mid_conv(a5, tmat, umat, gspec, order):
    nb, _, n1, n2, cw = a5.shape
    kb = _pick(n1, (MID_K1_PER_STEP,))
    return pl.pallas_call(
        functools.partial(_mid_kernel, conv=True),
        out_shape=jax.ShapeDtypeStruct((nb, 2, n1, n2, cw), BF16),
        grid=(n1 // kb, nb),
        in_specs=[pl.BlockSpec((None, 2, kb, n2, cw), lambda k, b: (b, 0, k, 0, 0)),
                  pl.BlockSpec((kb, 2 * n2, 2 * n2), lambda k, b: (k, 0, 0)),
                  pl.BlockSpec((kb, 2, n2, cw), lambda k, b: (k, 0, 0, order)),
                  pl.BlockSpec((kb, 2 * n2, 2 * n2), lambda k, b: (k, 0, 0))],
        out_specs=pl.BlockSpec((None, 2, kb, n2, cw), lambda k, b: (b, 0, k, 0, 0)),
        compiler_params=_cparams(("parallel", "arbitrary")),
        name="hyena_mid",
    )(a5, tmat, gspec, umat)


def _dft_tables(seq):
    n = 2 * seq
    n2 = DFT_N2
    n1 = n // n2
    two_pi = 2.0 * math.pi

    def cs(num, den):
        ang = (num % den).astype(F32) * (two_pi / den)
        return jnp.cos(ang), jnp.sin(ang)

    k1 = jnp.arange(n1, dtype=jnp.int32)
    c, s = cs(k1[:, None] * jnp.arange(n1, dtype=jnp.int32)[None, :], n1)
    f1_full = jnp.concatenate([c, -s], axis=0)
    i2 = jnp.arange(n2, dtype=jnp.int32)
    num = i2[None, None, :] * k1[:, None, None] + n1 * (i2[None, :, None] * i2[None, None, :])
    c, s = cs(num, n)
    tre, tim = c, -s
    tmat = jnp.concatenate([jnp.concatenate([tre, -tim], axis=2),
                            jnp.concatenate([tim, tre], axis=2)], axis=1)
    ure, uim = jnp.swapaxes(tre, 1, 2), -jnp.swapaxes(tim, 1, 2)
    umat = jnp.concatenate([jnp.concatenate([ure, -uim], axis=2),
                            jnp.concatenate([uim, ure], axis=2)], axis=1)
    c, s = cs(jnp.arange(n1 // 2, dtype=jnp.int32)[:, None] * k1[None, :], n1)
    fi = jnp.concatenate([c, -s], axis=1) * (1.0 / n)
    return (f1_full.astype(BF16), f1_full[:, :n1 // 2].astype(BF16), tmat.astype(BF16), umat.astype(BF16),
            fi.astype(BF16))


def _dft_tables_small(seq):
    n = 2 * seq
    k = jnp.arange(n, dtype=jnp.int32)
    ang = ((k[:, None] * k[None, :]) % n).astype(F32) * (2.0 * math.pi / n)
    c, s = jnp.cos(ang), jnp.sin(ang)
    fwd = jnp.concatenate([c, -s], axis=0)
    inv = jnp.concatenate([c[:seq], -s[:seq]], axis=1) * (1.0 / n)
    return fwd.astype(BF16), fwd[:, :seq].astype(BF16), inv.astype(BF16)


def _ctxconv_kernel(z_ref, xg_ref, taps_ref, asum_ref, bias_ref, ff_ref, fh_ref, fi_ref, o_ref):
    n = ff_ref.shape[1]
    z = z_ref[...]
    g = jnp.dot(ff_ref[...], taps_ref[...].astype(BF16), preferred_element_type=F32) / (asum_ref[...] + 1e-6)
    xk = jnp.dot(fh_ref[...], z, preferred_element_type=F32)
    xr, xi = xk[0:n], xk[n:]
    gr, gi = g[0:n], g[n:]
    yk = jnp.concatenate([xr * gr - xi * gi, xr * gi + xi * gr], axis=0).astype(BF16)
    y = jnp.dot(fi_ref[...], yk, preferred_element_type=F32)
    o_ref[...] = (xg_ref[...].astype(F32) * (y + bias_ref[...] * z.astype(F32))).astype(o_ref.dtype)


def _ctx_longconv(z, xg, taps, asum, bias, tabs, order, *, bsz, seq):
    ff, fh, fi = tabs
    n = 2 * seq
    cw = BRANCH_W
    return pl.pallas_call(
        _ctxconv_kernel,
        out_shape=jax.ShapeDtypeStruct((bsz * seq, cw), BF16),
        grid=(bsz,),
        in_specs=[pl.BlockSpec((seq, cw), lambda b: (b, 0)),
                  pl.BlockSpec((seq, cw), lambda b: (b, 0)),
                  pl.BlockSpec((n, cw), lambda b: (0, order)),
                  pl.BlockSpec((1, cw), lambda b: (0, order)),
                  pl.BlockSpec((1, cw), lambda b: (0, 0)),
                  pl.BlockSpec((2 * n, n), lambda b: (0, 0)),
                  pl.BlockSpec((2 * n, seq), lambda b: (0, 0)),
                  pl.BlockSpec((seq, 2 * n), lambda b: (0, 0))],
        out_specs=pl.BlockSpec((seq, cw), lambda b: (b, 0)),
        compiler_params=_cparams(("parallel",)),
        name="hyena_ctx_conv",
    )(z, xg, taps, asum, bias, ff, fh, fi)


def _hyena_long(v0, x1, x2, taps, asum, hy_bias, tabs, *, bsz, seq):
    f1_full, f1_half, tmat, umat, fi = tabs
    n2 = DFT_N2
    n1 = 2 * seq // n2
    cw = BRANCH_W
    tn = 4096
    at = _lmm(f1_full, taps.reshape(1, n1, n2 * 2 * cw), tn=tn, out_dtype=BF16, name="taps_dft1")
    gspec = _mid_spectrum(at.reshape(1, 2, n1, n2, 2 * cw), tmat, asum)

    def conv(z, xg, order):
        z2 = z.reshape(bsz, n1 // 2, n2 * cw)
        a = _lmm(f1_half, z2, tn=tn, out_dtype=BF16, name="hyena_dft1")
        bk = _mid_conv(a.reshape(bsz, 2, n1, n2, cw), tmat, umat, gspec, order)
        bias_t = jnp.tile(hy_bias[order].reshape(1, cw), (1, tn // cw))
        y = _lmm(fi, bk.reshape(bsz, 2 * n1, n2 * cw), tn=tn, out_dtype=BF16,
                 gate_args=(xg.reshape(bsz, n1 // 2, n2 * cw), z2, bias_t), name="hyena_idft2")
        return y.reshape(bsz * seq, cw)

    return conv(conv(v0, x1, 0), x2, 1)


def _s5_kernel(u_ref, bre_ref, bim_ref, cre_ref, cim_ref, are_ref, aim_ref, s0_ref,
               y_ref, sfin_ref,
               uf, lhs, bur, bui, ybuf, pw_r, pw_i, car_r, car_i, st_r, st_i, *, tseg, rev):
    i = pl.program_id(1)
    sw = S5_SW
    gw = sw // S5_SUPER
    a_re = are_ref[...]
    a_im = aim_ref[...]

    @pl.when(jnp.logical_and(pl.program_id(0) == 0, i == 0))
    def _():
        def pbody(r, carry):
            cr, ci = carry
            pw_r[pl.ds(r, 1), :] = cr
            pw_i[pl.ds(r, 1), :] = ci
            return cr * a_re - ci * a_im, cr * a_im + ci * a_re
        lax.fori_loop(0, tseg, pbody, (a_re, a_im))

    @pl.when(i == 0)
    def _():
        st_r[...] = s0_ref[:, 0:sw]
        st_i[...] = s0_ref[:, sw:]

    for sg in range(S5_SUPER):
        uf[sg] = u_ref[:, sg * 128:(sg + 1) * 128].astype(F32)

    def gather(r, _):
        for sg in range(S5_SUPER):
            lhs[sg, pl.ds(pl.multiple_of(r * 8, 8), 8), :] = uf[sg, pl.ds(r, 8, stride=tseg), :]
        return 0
    lax.fori_loop(0, tseg, gather, 0)

    for sg in range(S5_SUPER):
        lb = lhs[sg].astype(BF16)
        bur[:, sg * gw:(sg + 1) * gw] = jnp.dot(lb, bre_ref[sg], preferred_element_type=F32)
        bui[:, sg * gw:(sg + 1) * gw] = jnp.dot(lb, bim_ref[sg], preferred_element_type=F32)

    for sg in range(S5_SUPER):
        cols = slice(sg * gw, (sg + 1) * gw)
        ar = jnp.broadcast_to(a_re[:, cols], (8, gw))
        ai = jnp.broadcast_to(a_im[:, cols], (8, gw))

        def sbody(k, carry, cols=cols, ar=ar, ai=ai):
            sr, si = carry
            r = (tseg - 1 - k) if rev else k
            rows = pl.ds(pl.multiple_of(r * 8, 8), 8)
            nr = ar * sr - ai * si + bur[rows, cols]
            ni = ar * si + ai * sr + bui[rows, cols]
            bur[rows, cols] = nr
            bui[rows, cols] = ni
            return nr, ni
        zero = jnp.zeros((8, gw), F32)
        lax.fori_loop(0, tseg, sbody, (zero, zero))

    at_r = pw_r[tseg - 1:tseg, :]
    at_i = pw_i[tseg - 1:tseg, :]
    end_row = 0 if rev else (tseg - 1) * 8
    cr = st_r[...]
    ci = st_i[...]
    order = range(7, -1, -1) if rev else range(8)
    for s in order:
        car_r[s:s + 1, :] = cr
        car_i[s:s + 1, :] = ci
        er = bur[end_row + s:end_row + s + 1, :]
        ei = bui[end_row + s:end_row + s + 1, :]
        cr, ci = er + at_r * cr - at_i * ci, ei + at_r * ci + at_i * cr
    st_r[...] = cr
    st_i[...] = ci
    sfin_ref[:, 0:sw] = cr
    sfin_ref[:, sw:] = ci

    for sg in range(S5_SUPER):
        cols = slice(sg * gw, (sg + 1) * gw)
        kr = car_r[:, cols]
        kim = car_i[:, cols]

        def fbody(r, _, cols=cols, kr=kr, kim=kim):
            pidx = (tseg - 1 - r) if rev else r
            pr = pw_r[pl.ds(pidx, 1), cols]
            pi_ = pw_i[pl.ds(pidx, 1), cols]
            rows = pl.ds(pl.multiple_of(r * 8, 8), 8)
            bur[rows, cols] = bur[rows, cols] + (pr * kr - pi_ * kim)
            bui[rows, cols] = bui[rows, cols] + (pr * kim + pi_ * kr)
            return 0
        lax.fori_loop(0, tseg, fbody, 0)

    for sg in range(S5_SUPER):
        cols = slice(sg * gw, (sg + 1) * gw)
        ybuf[sg] = (jnp.dot(bur[:, cols].astype(BF16), cre_ref[sg], preferred_element_type=F32)
                    + jnp.dot(bui[:, cols].astype(BF16), cim_ref[sg], preferred_element_type=F32))

    def scatter(r, _):
        for sg in range(S5_SUPER):
            y_ref[sg, pl.ds(r, 8, stride=tseg), :] = ybuf[sg, pl.ds(pl.multiple_of(r * 8, 8), 8), :]
        return 0
    lax.fori_loop(0, tseg, scatter, 0)


def _s5_scan(proj, s5p, s0, *, bsz, seq, tseg, rev):
    bre, bim, cre, cim, are, aim = s5p
    tr = 8 * tseg
    nt = seq // tr
    sw = S5_SW
    gw = sw // S5_SUPER
    tile = (lambda b, i: (b * nt + (nt - 1 - i), U_U)) if rev else (lambda b, i: (b * nt + i, U_U))
    otile = (lambda b, i: (0, b * nt + (nt - 1 - i), 0)) if rev else (lambda b, i: (0, b * nt + i, 0))
    full = lambda shape: pl.BlockSpec(shape, lambda b, i: tuple(0 for _ in shape))
    return pl.pallas_call(
        functools.partial(_s5_kernel, tseg=tseg, rev=rev),
        out_shape=(jax.ShapeDtypeStruct((S5_SUPER, bsz * seq, 128), F32),
                   jax.ShapeDtypeStruct((bsz, 1, 2 * sw), F32)),
        grid=(bsz, nt),
        in_specs=[pl.BlockSpec((tr, UNIT), tile),
                  full((S5_SUPER, 128, gw)), full((S5_SUPER, 128, gw)),
                  full((S5_SUPER, gw, 128)), full((S5_SUPER, gw, 128)),
                  full((1, sw)), full((1, sw)),
                  pl.BlockSpec((None, 1, 2 * sw), lambda b, i: (b, 0, 0))],
        out_specs=(pl.BlockSpec((S5_SUPER, tr, 128), otile),
                   pl.BlockSpec((None, 1, 2 * sw), lambda b, i: (b, 0, 0))),
        scratch_shapes=[pltpu.VMEM((S5_SUPER, tr, 128), F32), pltpu.VMEM((S5_SUPER, tr, 128), F32),
                        pltpu.VMEM((tr, sw), F32), pltpu.VMEM((tr, sw), F32),
                        pltpu.VMEM((S5_SUPER, tr, 128), F32),
                        pltpu.VMEM((tseg, sw), F32), pltpu.VMEM((tseg, sw), F32),
                        pltpu.VMEM((8, sw), F32), pltpu.VMEM((8, sw), F32),
                        pltpu.VMEM((1, sw), F32), pltpu.VMEM((1, sw), F32)],
        compiler_params=_cparams(("arbitrary", "arbitrary")),
        name="s5_bwd" if rev else "s5_fwd",
    )(proj, bre, bim, cre, cim, are, aim, s0)


def _s5_params(a_re, a_im, log_step, b_re, b_im, c_re, c_im):
    g, p, ci = S5_GROUPS, S5_STATE, S5_GROUP
    dt = jnp.exp(log_step)[:, None]
    mag = jnp.exp(a_re * dt)
    ar, ai = mag * jnp.cos(a_im * dt), mag * jnp.sin(a_im * dt)
    den = a_re * a_re + a_im * a_im
    fr = ((ar - 1.0) * a_re + ai * a_im) / den
    fi = (ai * a_re - (ar - 1.0) * a_im) / den
    bbr = fr[..., None] * b_re - fi[..., None] * b_im
    bbi = fr[..., None] * b_im + fi[..., None] * b_re
    eye = jnp.eye(8, dtype=F32)

    def blockdiag_b(m):
        m4 = m.reshape(S5_SUPER, 8, p, ci)
        return jnp.einsum('sgpc,gh->sgchp', m4, eye).reshape(S5_SUPER, 8 * ci, 8 * p).astype(BF16)

    def blockdiag_c(m):
        m4 = m.reshape(S5_SUPER, 8, ci, p)
        return jnp.einsum('sgcp,gh->sgphc', m4, eye).reshape(S5_SUPER, 8 * p, 8 * ci).astype(BF16)

    return (blockdiag_b(bbr), blockdiag_b(bbi), blockdiag_c(c_re), blockdiag_c(-c_im),
            ar.reshape(1, g * p), ai.reshape(1, g * p))


def _merge_kernel(ya_ref, yh_ref, hg_ref, yatt_ref, attg_ref, ysf_ref, ysb_ref, u_ref, s5g_ref,
                  mg0_ref, mg1_ref, mg2_ref, mg3_ref, x_ref, mod_ref, s5d_ref, wglu_ref, wb_ref, wout_ref,
                  lng_ref, lnb_ref, o_ref, *, alpha):
    f = lambda r: r[...].astype(F32)
    y_a = f(ya_ref)
    y_h = f(yh_ref) * _silu(f(hg_ref))
    y_c = f(yatt_ref) * _silu(f(attg_ref))
    ys = jnp.concatenate([ysf_ref[sg] + ysb_ref[sg] for sg in range(S5_SUPER)], axis=1)
    y = ys + s5d_ref[...] * f(u_ref)
    zg = jax.nn.gelu(y)
    glu = jnp.dot(zg.astype(BF16), wglu_ref[...], preferred_element_type=F32)
    y_d = zg * _sigmoid(glu) * _silu(f(s5g_ref))
    mix = None
    for n, (yn, mg) in enumerate(((y_a, mg0_ref), (y_h, mg1_ref), (y_c, mg2_ref), (y_d, mg3_ref))):
        term = _sigmoid(f(mg)) * jnp.dot(yn.astype(BF16), wb_ref[n], preferred_element_type=F32)
        mix = term if mix is None else mix + term
    out = jnp.dot(mix.astype(BF16), wout_ref[...], preferred_element_type=F32)
    gate = mod_ref[2:3, :]
    r = alpha * x_ref[...] + gate * out
    mu = jnp.mean(r, axis=1, keepdims=True)
    rc = r - mu
    var = jnp.mean(rc * rc, axis=1, keepdims=True)
    o_ref[...] = rc * lax.rsqrt(var + LN_EPS) * lng_ref[...] + lnb_ref[...]


def _merge(ya, yh, yatt, ysf, ysb, proj, x2d, mod3, s5d, wglu, wb, wout, lng, lnb, *, seq, tm, alpha):
    m = x2d.shape[0]
    nt_seq = seq // tm
    nb = mod3.shape[0]
    mod_map = (lambda i: (i // nt_seq, 0, 0)) if nb > 1 else (lambda i: (0, 0, 0))
    row = lambda w_: pl.BlockSpec((tm, w_), lambda i: (i, 0))
    pcol = lambda unit: pl.BlockSpec((tm, UNIT), lambda i: (i, unit))
    mcol = lambda n: pl.BlockSpec((tm, 2 * UNIT), lambda i: (i, U_MERGE // 2 + n))
    full = lambda shape: pl.BlockSpec(shape, lambda i: tuple(0 for _ in shape))
    w_ = BRANCH_W
    s5row = pl.BlockSpec((S5_SUPER, tm, 128), lambda i: (0, i, 0))
    return pl.pallas_call(
        functools.partial(_merge_kernel, alpha=alpha),
        out_shape=jax.ShapeDtypeStruct((m, D_MODEL), F32),
        grid=(m // tm,),
        in_specs=[row(w_), row(w_), pcol(U_HG), row(w_), pcol(U_ATTG), s5row, s5row, pcol(U_U), pcol(U_S5G),
                  mcol(0), mcol(1), mcol(2), mcol(3), row(D_MODEL),
                  pl.BlockSpec((None, 3, D_MODEL), mod_map),
                  full((1, w_)), full((w_, w_)), full((N_BRANCH, w_, D_MODEL)), full((D_MODEL, D_MODEL)),
                  full((1, D_MODEL)), full((1, D_MODEL))],
        out_specs=row(D_MODEL),
        compiler_params=_cparams(("parallel",)),
        name="merge_out_norm",
    )(ya, yh, proj, yatt, proj, ysf, ysb, proj, proj, proj, proj, proj, proj, x2d, mod3,
      s5d, wglu, wb, wout, lng, lnb)


def _rope_tables(n_lat):
    rows = n_lat // GRID_W
    row = jnp.broadcast_to(jnp.arange(rows)[:, None], (rows, GRID_W)).reshape(-1)
    col = jnp.broadcast_to(jnp.arange(GRID_W)[None, :], (rows, GRID_W)).reshape(-1)
    half = HEAD_DIM // 2
    inv = 1.0 / (ROPE_BASE ** (jnp.arange(0, half, 2, dtype=F32) / half))
    ar, ac = row[:, None] * inv, col[:, None] * inv
    cos64 = jnp.concatenate([jnp.cos(ar), jnp.cos(ar), jnp.cos(ac), jnp.cos(ac)], axis=-1)
    sin64 = jnp.concatenate([-jnp.sin(ar), jnp.sin(ar), -jnp.sin(ac), jnp.sin(ac)], axis=-1)
    return jnp.tile(cos64, (1, 2)), jnp.tile(sin64, (1, 2))


def _pick(n, prefs):
    for t in prefs:
        if n % t == 0:
            return t
    return n


def kernel(x, c, ctx, c_ctx, w_mod, b_mod, w_in, conv_a, conv_h, hy_w1, hy_b1, hy_w2, hy_b2, hy_w3, hy_freq,
           hy_delta, hy_bias, lam_q1, lam_k1, lam_q2, lam_k2, attn_norm_g, s5_a_re, s5_a_im, s5_log_step,
           s5_b_re, s5_b_im, s5_c_re, s5_c_im, s5_d, s5_w_glu, w_branch, w_out, ln_g, ln_b):
    bsz, seq, d = x.shape
    seq_c = ctx.shape[1]
    depth = w_in.shape[0]
    alpha = (2.0 * depth) ** 0.25
    assert d == D_MODEL and seq % 1024 == 0 and seq_c % 256 == 0 and bsz + 1 <= 8

    cos_t, sin_t = _rope_tables(seq)
    dft_lat = _dft_tables(seq)
    dft_ctx = _dft_tables_small(seq_c)
    cvec = jnp.zeros((8, d), F32).at[0:bsz].set(c).at[bsz].set(c_ctx)
    tm_in = _pick(seq, (2048, 1024))
    tm_el = _pick(seq, (512,))
    tq = _pick(seq, (256,))
    ck = _pick(seq, (1024, 512))
    tseg_lat = 128
    tseg_ctx = seq_c // 8

    x2 = x.reshape(bsz * seq, d)
    xc2 = ctx.reshape(bsz * seq_c, d)
    for l in range(depth):
        last = l == depth - 1
        lam_init = 0.8 - 0.6 * math.exp(-0.3 * l)
        mod = _modulation(cvec, w_mod, b_mod, l).reshape(8, 3, d)
        mod_lat, mod_ctx = mod[0:bsz], mod[bsz:bsz + 1]
        w_l = w_in[l].reshape(d, N_UNITS, UNIT)[:, jnp.array(UNIT_PERM)].reshape(d, PROJ_W).astype(BF16)
        lamp = jnp.stack([lam_q1[l], lam_k1[l], lam_q2[l], lam_k2[l]], axis=0)
        g_att = attn_norm_g[l].reshape(1, V_DIM)
        s5f = _s5_params(s5_a_re[l, 0], s5_a_im[l, 0], s5_log_step[l, 0], s5_b_re[l, 0], s5_b_im[l, 0],
                         s5_c_re[l, 0], s5_c_im[l, 0])
        s5b = _s5_params(s5_a_re[l, 1], s5_a_im[l, 1], s5_log_step[l, 1], s5_b_re[l, 1], s5_b_im[l, 1],
                         s5_c_re[l, 1], s5_c_im[l, 1])
        hp = (hy_w1[l], hy_b1[l], hy_w2[l], hy_b2[l], hy_w3[l], hy_freq[l], hy_delta[l])
        wglu = s5_w_glu[l].astype(BF16)
        wb = w_branch[l].astype(BF16)
        wout = w_out[l].astype(BF16)
        s5d = s5_d[l].reshape(1, BRANCH_W)
        lng, lnb = ln_g[l].reshape(1, d), ln_b[l].reshape(1, d)

        ncols_c = 3 * 1024 if last else PROJ_W
        projc = _inproj(xc2, mod_ctx, w_l[:, :ncols_c], cos_t, sin_t, seq=seq_c, rope=False,
                        tm=bsz * seq_c, ncols=ncols_c)
        zero_state = jnp.zeros((bsz, 1, 2 * S5_SW), F32)
        ycf, scf = _s5_scan(projc, s5f, zero_state, bsz=bsz, seq=seq_c, tseg=tseg_ctx, rev=False)
        ycb, scb = _s5_scan(projc, s5b, zero_state, bsz=bsz, seq=seq_c, tseg=tseg_ctx, rev=True)

        proj = _inproj(x2, mod_lat, w_l, cos_t, sin_t, seq=seq, rope=True, tm=tm_in, ncols=PROJ_W)
        yatt = _attention_lat(proj, projc, lamp, g_att, bsz=bsz, seq=seq, seq_c=seq_c,
                              lam_init=lam_init, tq=tq, nsub=2, ck=ck)
        ysf, _ = _s5_scan(proj, s5f, scf, bsz=bsz, seq=seq, tseg=tseg_lat, rev=False)
        ysb, _ = _s5_scan(proj, s5b, scb, bsz=bsz, seq=seq, tseg=tseg_lat, rev=True)
        ya, v0, x1, x2h = _convgate(proj, conv_a[l], conv_h[l], bsz=bsz, seq=seq, tm=tm_el)
        taps, asum = _hyena_taps(hp, seq=seq, tr=512)
        yh = _hyena_long(v0, x1, x2h, taps, asum, hy_bias[l], dft_lat, bsz=bsz, seq=seq)
        x_new = _merge(ya, yh, yatt, ysf, ysb, proj, x2, mod_lat, s5d, wglu, wb, wout, lng, lnb,
                       seq=seq, tm=tm_el, alpha=alpha)

        if not last:
            yatt_c = _attention(projc, None, projc, lamp, g_att, bsz=bsz, seq_q=seq_c, seq_kv=0, seq_c=seq_c,
                                lam_init=lam_init, tq=seq_c, tk=seq_c)
            ya_c, v0c, x1c, x2c = _convgate(projc, conv_a[l], conv_h[l], bsz=bsz, seq=seq_c, tm=seq_c)
            taps_c, asum_c = _hyena_taps(hp, seq=seq_c, tr=seq_c)
            b0 = hy_bias[l, 0].reshape(1, BRANCH_W)
            b1 = hy_bias[l, 1].reshape(1, BRANCH_W)
            z1c = _ctx_longconv(v0c, x1c, taps_c, asum_c, b0, dft_ctx, 0, bsz=bsz, seq=seq_c)
            yh_c = _ctx_longconv(z1c, x2c, taps_c, asum_c, b1, dft_ctx, 1, bsz=bsz, seq=seq_c)
            xc2 = _merge(ya_c, yh_c, yatt_c, ycf, ycb, projc, xc2, mod_ctx, s5d, wglu, wb, wout, lng, lnb,
                         seq=seq_c, tm=seq_c, alpha=alpha)
        x2 = x_new
    return x2.reshape(bsz, seq, d)
```

```python
import functools
import math

import jax
import jax.numpy as jnp
from jax import lax
from jax.experimental import pallas as pl
from jax.experimental.pallas import tpu as pltpu

F32 = jnp.float32
BF16 = jnp.bfloat16
HIGHEST = lax.Precision.HIGHEST

D_MODEL = 1024
BRANCH_W = 512
N_HEADS = 4
HEAD_DIM = 64
V_DIM = 128
GRID_W = 64
ROPE_BASE = 10000.0
HYENA_BANDS = 16
HYENA_HIDDEN = 64
HYENA_SHIFT = 0.05
S5_GROUP = 16
S5_GROUPS = 32
S5_STATE = 64
S5_SUPER = 4
S5_SW = S5_GROUPS * S5_STATE
LN_EPS = 1e-5
N_BRANCH = 4
UNIT = 512
N_UNITS = 22
PROJ_W = N_UNITS * UNIT
UNIT_PERM = (0, 1, 11, 12, 2, 4, 5, 7, 8, 9, 3, 6, 13, 10, 14, 15, 16, 17, 18, 19, 20, 21)
U_K, U_V, U_Q, U_ATTG, U_U, U_CONV, U_AB, U_S5G, U_HG, U_MERGE = 0, 1, 2, 3, 4, 5, 10, 12, 13, 14
DFT_N2 = 128
MID_K1_PER_STEP = 8
S5_UNROLL = 8
VMEM_LIMIT = 56 * 1024 * 1024


def _cparams(sem):
    return pltpu.CompilerParams(dimension_semantics=sem, vmem_limit_bytes=VMEM_LIMIT)


def _sigmoid(v):
    return 0.5 * jnp.tanh(0.5 * v) + 0.5


def _silu(v):
    return v * _sigmoid(v)


def _mod_kernel(s_ref, w_ref, b_ref, o_ref):
    s = _silu(s_ref[...])
    o_ref[...] = jnp.dot(s, w_ref[...], preferred_element_type=F32, precision=HIGHEST) + b_ref[...]


def _modulation(cvec, w_mod, b_mod, layer):
    depth, _, n = w_mod.shape
    tn = 512
    return pl.pallas_call(
        _mod_kernel,
        out_shape=jax.ShapeDtypeStruct((8, n), F32),
        grid=(n // tn,),
        in_specs=[pl.BlockSpec((8, D_MODEL), lambda j: (0, 0)),
                  pl.BlockSpec((None, D_MODEL, tn), lambda j: (layer, 0, j)),
                  pl.BlockSpec((None, 1, tn), lambda j: (layer, 0, j))],
        out_specs=pl.BlockSpec((8, tn), lambda j: (0, j)),
        compiler_params=_cparams(("parallel",)),
        name="modulation",
    )(cvec, w_mod, b_mod.reshape(depth, 1, n))


def _inproj_kernel(x_ref, mod_ref, w_ref, cos_ref, sin_ref, o_ref, h_ref, *, rope):
    j = pl.program_id(1)

    @pl.when(j == 0)
    def _():
        shift = mod_ref[0:1, :]
        scale = mod_ref[1:2, :]
        h_ref[...] = (x_ref[...] * (1.0 + scale) + shift).astype(BF16)

    def project():
        return jnp.dot(h_ref[...], w_ref[...], preferred_element_type=F32)

    if not rope:
        o_ref[...] = project().astype(o_ref.dtype)
        return

    @pl.when(j < 2)
    def _():
        acc = project()
        cs = cos_ref[...]
        sn = sin_ref[...]
        lane = lax.broadcasted_iota(jnp.int32, cs.shape, 1)
        first = (lane % 32) < 16
        for cb in range(UNIT // 128):
            t = acc[:, cb * 128:(cb + 1) * 128]
            partner = jnp.where(first, pltpu.roll(t, 128 - 16, 1), pltpu.roll(t, 16, 1))
            o_ref[:, cb * 128:(cb + 1) * 128] = (t * cs + partner * sn).astype(o_ref.dtype)
        o_ref[:, UNIT:] = acc[:, UNIT:].astype(o_ref.dtype)

    @pl.when(j >= 2)
    def _():
        o_ref[...] = project().astype(o_ref.dtype)


def _inproj(x2d, mod3, w, cos_t, sin_t, *, seq, rope, tm, ncols):
    m = x2d.shape[0]
    tn = 1024
    nt_seq = max(seq // tm, 1)
    nb = mod3.shape[0]
    mod_map = (lambda i, j: (i // nt_seq, 0, 0)) if nb > 1 else (lambda i, j: (0, 0, 0))
    return pl.pallas_call(
        functools.partial(_inproj_kernel, rope=rope),
        out_shape=jax.ShapeDtypeStruct((m, ncols), BF16),
        grid=(m // tm, ncols // tn),
        in_specs=[pl.BlockSpec((tm, D_MODEL), lambda i, j: (i, 0)),
                  pl.BlockSpec((None, 3, D_MODEL), mod_map),
                  pl.BlockSpec((D_MODEL, tn), lambda i, j: (0, j)),
                  pl.BlockSpec((tm, 128), lambda i, j: (i % nt_seq, 0)),
                  pl.BlockSpec((tm, 128), lambda i, j: (i % nt_seq, 0))],
        out_specs=pl.BlockSpec((tm, tn), lambda i, j: (i, j)),
        scratch_shapes=[pltpu.VMEM((tm, D_MODEL), BF16)],
        compiler_params=_cparams(("parallel", "arbitrary")),
        name="inproj_rope" if rope else "inproj_ctx",
    )(x2d, mod3, w, cos_t, sin_t)


def _head_output(o1, o2, lamp_ref, g_ref, lam_init):
    lp = lamp_ref[...]
    lam = (jnp.exp(jnp.sum(lp[0:1] * lp[1:2], axis=1, keepdims=True))
           - jnp.exp(jnp.sum(lp[2:3] * lp[3:4], axis=1, keepdims=True)) + lam_init)
    od = o1 - lam * o2
    return od * lax.rsqrt(jnp.mean(od * od, axis=1, keepdims=True) + 1e-5) * g_ref[...] * (1.0 - lam_init)


def _attn_ctx_kernel(q_ref, kc_ref, vc_ref, lamp_ref, g_ref, o_ref, *, tq, lam_init):
    q = q_ref[...].astype(F32) * (HEAD_DIM ** -0.5)
    lane = lax.broadcasted_iota(jnp.int32, q.shape, 1)
    qz = jnp.concatenate([jnp.where(lane < HEAD_DIM, q, 0.0), jnp.where(lane >= HEAD_DIM, q, 0.0)],
                         axis=0).astype(BF16)
    s = lax.dot_general(qz, kc_ref[...], (((1,), (1,)), ((), ())), preferred_element_type=F32)
    p = jnp.exp(s - jnp.max(s, axis=1, keepdims=True))
    l = jnp.sum(p, axis=1, keepdims=True)
    o = jnp.dot(p.astype(BF16), vc_ref[...], preferred_element_type=F32) / l
    o_ref[...] = _head_output(o[0:tq], o[tq:], lamp_ref, g_ref, lam_init).astype(o_ref.dtype)


def _attention_ctx(proj_c, lamp, g, *, bsz, seq_c, lam_init):
    hq = U_Q * UNIT // 128
    hk = U_K * UNIT // 128
    hv = U_V * UNIT // 128
    return pl.pallas_call(
        functools.partial(_attn_ctx_kernel, tq=seq_c, lam_init=lam_init),
        out_shape=jax.ShapeDtypeStruct((bsz * seq_c, N_HEADS * V_DIM), BF16),
        grid=(bsz, N_HEADS),
        in_specs=[pl.BlockSpec((seq_c, 128), lambda b, h: (b, hq + h)),
                  pl.BlockSpec((seq_c, 128), lambda b, h: (b, hk + h)),
                  pl.BlockSpec((seq_c, 128), lambda b, h: (b, hv + h)),
                  pl.BlockSpec((4, HEAD_DIM), lambda b, h: (0, 0)),
                  pl.BlockSpec((1, V_DIM), lambda b, h: (0, 0))],
        out_specs=pl.BlockSpec((seq_c, V_DIM), lambda b, h: (b, h)),
        compiler_params=_cparams(("parallel", "parallel")),
        name="diff_attn_ctx",
    )(proj_c, proj_c, proj_c, lamp, g)


def _attn_lat_kernel(q_ref, k_ref, v_ref, kc_ref, vc_ref, lamp_ref, g_ref, o_ref,
                     qz, vext, vcext, sbuf, pbuf, abuf, m_s, acc, *, tq, nsub, ck, n_chunks, lam_init):
    @pl.when(pl.program_id(2) == 0)
    def _():
        vext[:, 0:V_DIM] = v_ref[...]
        vext[:, V_DIM:] = jnp.ones((vext.shape[0], V_DIM), BF16)
        vcext[:, 0:V_DIM] = vc_ref[...]
        vcext[:, V_DIM:] = jnp.ones((vcext.shape[0], V_DIM), BF16)

    subs = range(nsub)
    for h in subs:
        q = q_ref[h * tq:(h + 1) * tq, :].astype(F32) * (HEAD_DIM ** -0.5 * math.log2(math.e))
        lane = lax.broadcasted_iota(jnp.int32, q.shape, 1)
        qz[h, 0:tq, :] = jnp.where(lane < HEAD_DIM, q, 0.0).astype(BF16)
        qz[h, tq:, :] = jnp.where(lane >= HEAD_DIM, q, 0.0).astype(BF16)

    def qk(h, kblk):
        return lax.dot_general(qz[h], kblk, (((1,), (1,)), ((), ())), preferred_element_type=F32)

    def kchunk(c):
        return k_ref[pl.ds(pl.multiple_of(c * ck, ck), ck), :]

    def vchunk(c):
        return vext[pl.ds(pl.multiple_of(c * ck, ck), ck), :]

    def score(slot, c):
        kblk = kchunk(c)
        for h in subs:
            sbuf[h, slot] = qk(h, kblk)

    def softmax(slot):
        for h in subs:
            s = sbuf[h, slot]
            m_prev = m_s[h]
            m_new = jnp.maximum(m_prev, jnp.max(s, axis=1, keepdims=True))
            abuf[h, slot] = jnp.exp2(m_prev - m_new)
            pbuf[h, slot] = jnp.exp2(s - m_new).astype(BF16)
            m_s[h] = m_new

    def pv(slot, c):
        vblk = vchunk(c)
        for h in subs:
            acc[h] = abuf[h, slot] * acc[h] + jnp.dot(pbuf[h, slot], vblk, preferred_element_type=F32)

    for h in subs:
        s = qk(h, kc_ref[...])
        m0 = jnp.max(s, axis=1, keepdims=True)
        m_s[h] = m0
        acc[h] = jnp.dot(jnp.exp2(s - m0).astype(BF16), vcext[...], preferred_element_type=F32)

    score(0, 0)
    softmax(0)
    score(1, 1)

    def body(j, _):
        a = 2 * j
        pv(0, a)
        score(0, a + 2)
        softmax(1)
        pv(1, a + 1)
        score(1, a + 3)
        softmax(0)
        return 0
    lax.fori_loop(0, n_chunks // 2 - 1, body, 0)

    pv(0, n_chunks - 2)
    softmax(1)
    pv(1, n_chunks - 1)

    for h in subs:
        a_ = acc[h]
        o = a_[:, 0:V_DIM] / a_[:, V_DIM:]
        o_ref[h * tq:(h + 1) * tq, :] = _head_output(o[0:tq], o[tq:], lamp_ref, g_ref,
                                                     lam_init).astype(o_ref.dtype)


def _attention_lat(proj, proj_c, lamp, g, *, bsz, seq, seq_c, lam_init, tq, nsub, ck):
    tqs = tq * nsub
    nq = seq // tqs
    n_chunks = seq // ck
    assert n_chunks % 2 == 0 and n_chunks >= 2
    hq = U_Q * UNIT // 128
    hk = U_K * UNIT // 128
    hv = U_V * UNIT // 128
    return pl.pallas_call(
        functools.partial(_attn_lat_kernel, tq=tq, nsub=nsub, ck=ck, n_chunks=n_chunks, lam_init=lam_init),
        out_shape=jax.ShapeDtypeStruct((bsz * seq, N_HEADS * V_DIM), BF16),
        grid=(bsz, N_HEADS, nq),
        in_specs=[pl.BlockSpec((tqs, 128), lambda b, h, qi: (b * nq + qi, hq + h)),
                  pl.BlockSpec((seq, 128), lambda b, h, qi: (b, hk + h)),
                  pl.BlockSpec((seq, 128), lambda b, h, qi: (b, hv + h)),
                  pl.BlockSpec((seq_c, 128), lambda b, h, qi: (b, hk + h)),
                  pl.BlockSpec((seq_c, 128), lambda b, h, qi: (b, hv + h)),
                  pl.BlockSpec((4, HEAD_DIM), lambda b, h, qi: (0, 0)),
                  pl.BlockSpec((1, V_DIM), lambda b, h, qi: (0, 0))],
        out_specs=pl.BlockSpec((tqs, V_DIM), lambda b, h, qi: (b * nq + qi, h)),
        scratch_shapes=[pltpu.VMEM((nsub, 2 * tq, 128), BF16),
                        pltpu.VMEM((seq, 2 * V_DIM), BF16), pltpu.VMEM((seq_c, 2 * V_DIM), BF16),
                        pltpu.VMEM((nsub, 2, 2 * tq, ck), F32), pltpu.VMEM((nsub, 2, 2 * tq, ck), BF16),
                        pltpu.VMEM((nsub, 2, 2 * tq, 1), F32), pltpu.VMEM((nsub, 2 * tq, 1), F32),
                        pltpu.VMEM((nsub, 2 * tq, 2 * V_DIM), F32)],
        compiler_params=_cparams(("parallel", "parallel", "arbitrary")),
        name="diff_attn",
    )(proj, proj, proj, proj_c, proj_c, lamp, g)


def _conv3(cur, prv, nxt, w):
    tm = cur.shape[0]
    row = lax.broadcasted_iota(jnp.int32, cur.shape, 0)
    dn = jnp.where(row == 0, prv, pltpu.roll(cur, 1, 0))
    up = jnp.where(row == tm - 1, nxt, pltpu.roll(cur, tm - 1, 0))
    return dn * w[0:1] + cur * w[1:2] + up * w[2:3]


def _convgate_kernel(main_ref, prev_ref, next_ref, ab_ref, ca_ref, ch_ref,
                     ya_ref, v0_ref, x1_ref, x2_ref, *, nt):
    i = pl.program_id(1)
    w_ = BRANCH_W
    m = main_ref[...].astype(F32)
    pv = jnp.where(i > 0, prev_ref[15:16, :].astype(F32), 0.0)
    nx = jnp.where(i < nt - 1, next_ref[0:1, :].astype(F32), 0.0)
    ab = ab_ref[...].astype(F32)
    p = m[:, 0:w_] * m[:, w_:2 * w_]
    p_prev = pv[:, 0:w_] * pv[:, w_:2 * w_]
    p_next = nx[:, 0:w_] * nx[:, w_:2 * w_]
    ya = ab[:, 0:w_] * _conv3(p, p_prev, p_next, ca_ref[...]) * _silu(ab[:, w_:])
    ya_ref[...] = ya.astype(ya_ref.dtype)
    ch = ch_ref[...]
    for n, ref in enumerate((v0_ref, x1_ref, x2_ref)):
        lo, hi = (2 + n) * w_, (3 + n) * w_
        ref[...] = _conv3(m[:, lo:hi], pv[:, lo:hi], nx[:, lo:hi], ch[:, n * w_:(n + 1) * w_]).astype(ref.dtype)


def _convgate(proj, conv_a, conv_h, *, bsz, seq, tm):
    m = bsz * seq
    nt = seq // tm
    cw = 5 * UNIT
    hb = 16
    nhb = m // hb
    out = jax.ShapeDtypeStruct((m, BRANCH_W), BF16)
    ospec = pl.BlockSpec((tm, BRANCH_W), lambda b, i: (b * nt + i, 0))
    return pl.pallas_call(
        functools.partial(_convgate_kernel, nt=nt),
        out_shape=(out, out, out, out),
        grid=(bsz, nt),
        in_specs=[pl.BlockSpec((tm, cw), lambda b, i: (b * nt + i, U_CONV * UNIT // cw)),
                  pl.BlockSpec((hb, cw), lambda b, i: (jnp.maximum((b * nt + i) * (tm // hb) - 1, 0), 1)),
                  pl.BlockSpec((hb, cw), lambda b, i: (jnp.minimum((b * nt + i + 1) * (tm // hb), nhb - 1), 1)),
                  pl.BlockSpec((tm, 2 * UNIT), lambda b, i: (b * nt + i, U_AB * UNIT // (2 * UNIT))),
                  pl.BlockSpec((3, BRANCH_W), lambda b, i: (0, 0)),
                  pl.BlockSpec((3, 3 * BRANCH_W), lambda b, i: (0, 0))],
        out_specs=(ospec, ospec, ospec, ospec),
        compiler_params=_cparams(("parallel", "parallel")),
        name="convgate",
    )(proj, proj, proj, proj, conv_a, conv_h)


def _taps_kernel(cols_ref, w1c_ref, w1s_ref, w2_ref, w3_ref, dl_ref, taps_ref, asum_ref, *, seq, tr):
    i = pl.program_id(0)

    def source(m):
        return jnp.where(m < seq, m, 2 * seq - m).astype(F32)

    src_l = source(i * tr + lax.broadcasted_iota(jnp.int32, (1, tr), 1))
    t_l = src_l / (seq - 1.0)
    w_l = (2.0 * math.pi / seq) * src_l
    band = lax.broadcasted_iota(jnp.int32, (HYENA_BANDS, 1), 0).astype(F32)
    f = 1e-4 + band * ((HYENA_BANDS - 1.0 - 1e-4) / (HYENA_BANDS - 1.0))
    ang = f * w_l
    cols = cols_ref[...]
    pre = (cols[:, 0:1] * t_l
           + jnp.dot(w1c_ref[...], jnp.cos(ang), preferred_element_type=F32, precision=HIGHEST)
           + jnp.dot(w1s_ref[...], -jnp.sin(ang), preferred_element_type=F32, precision=HIGHEST)
           + cols[:, 1:2])
    h = jnp.sin(cols[:, 3:4] * pre)
    h = jnp.sin(cols[:, 4:5] * (jnp.dot(w2_ref[...], h, preferred_element_type=F32, precision=HIGHEST)
                                + cols[:, 2:3]))
    h = lax.dot_general(h, w3_ref[...], (((0,), (0,)), ((), ())), preferred_element_type=F32,
                        precision=HIGHEST)
    mrow = i * tr + lax.broadcasted_iota(jnp.int32, (tr, 1), 0)
    t = source(mrow) / (seq - 1.0)
    decay = jnp.exp(-t * jnp.abs(dl_ref[...]))
    out = h * (decay + HYENA_SHIFT)
    out = jnp.where(mrow == seq, 0.0, out)
    taps_ref[...] = out

    @pl.when(i == 0)
    def _():
        asum_ref[...] = jnp.zeros(asum_ref.shape, F32)

    asum_ref[...] += jnp.sum(jnp.abs(out), axis=0, keepdims=True)


def _hyena_taps(hp, *, seq, tr):
    w1, b1, w2, b2, w3, freq, deltas = hp
    hh = HYENA_HIDDEN
    cw = 2 * BRANCH_W
    w3d = w3.reshape(hh, 2, 2, BRANCH_W).transpose(2, 0, 1, 3).reshape(2, hh, cw)
    dld = deltas.transpose(1, 0, 2).reshape(2, 1, cw)
    cols = jnp.stack([w1[0], b1, b2, freq[0], freq[1]], axis=1)
    nt = 2 * seq // tr
    full = lambda shape: pl.BlockSpec(shape, lambda i: tuple(0 for _ in shape))
    return pl.pallas_call(
        functools.partial(_taps_kernel, seq=seq, tr=tr),
        out_shape=(jax.ShapeDtypeStruct((2 * seq, cw), F32), jax.ShapeDtypeStruct((1, cw), F32)),
        grid=(nt,),
        in_specs=[full((hh, 5)), full((hh, HYENA_BANDS)), full((hh, HYENA_BANDS)), full((hh, hh)),
                  pl.BlockSpec((None, hh, cw), lambda i: (i // (nt // 2), 0, 0)),
                  pl.BlockSpec((None, 1, cw), lambda i: (i // (nt // 2), 0, 0))],
        out_specs=(pl.BlockSpec((tr, cw), lambda i: (i, 0)), pl.BlockSpec((1, cw), lambda i: (0, 0))),
        compiler_params=_cparams(("arbitrary",)),
        name="hyena_taps",
    )(cols, w1[1:1 + HYENA_BANDS].T, w1[1 + HYENA_BANDS:].T, w2.T, w3d, dld)


def _lmm_kernel(a_ref, x_ref, *rest, gate):
    if gate:
        xg_ref, z_ref, bias_ref, o_ref = rest
    else:
        (o_ref,) = rest
    acc = jnp.dot(a_ref[...], x_ref[...].astype(BF16), preferred_element_type=F32)
    if gate:
        acc = xg_ref[...].astype(F32) * (acc + bias_ref[...] * z_ref[...].astype(F32))
    o_ref[...] = acc.astype(o_ref.dtype)


def _lmm(a, x, *, tn, out_dtype, gate_args=None, name):
    nb, k, n = x.shape
    ma = a.shape[0]
    in_specs = [pl.BlockSpec((ma, k), lambda b, j: (0, 0)), pl.BlockSpec((None, k, tn), lambda b, j: (b, 0, j))]
    args = [a, x]
    if gate_args is not None:
        xg, z, bias_t = gate_args
        in_specs += [pl.BlockSpec((None, ma, tn), lambda b, j: (b, 0, j)),
                     pl.BlockSpec((None, ma, tn), lambda b, j: (b, 0, j)),
                     pl.BlockSpec((1, tn), lambda b, j: (0, 0))]
        args += [xg, z, bias_t]
    return pl.pallas_call(
        functools.partial(_lmm_kernel, gate=gate_args is not None),
        out_shape=jax.ShapeDtypeStruct((nb, ma, n), out_dtype),
        grid=(nb, n // tn),
        in_specs=in_specs,
        out_specs=pl.BlockSpec((None, ma, tn), lambda b, j: (b, 0, j)),
        compiler_params=_cparams(("parallel", "parallel")),
        name=name,
    )(*args)


def _mid_kernel(a_ref, t_ref, *rest, conv):
    if conv:
        g_ref, u_ref, o_ref = rest
    else:
        asum_ref, o_ref = rest
    kb, n2 = a_ref.shape[1], a_ref.shape[2]
    for kk in range(kb):
        a = jnp.concatenate([a_ref[0, kk], a_ref[1, kk]], axis=0)
        xk = jnp.dot(t_ref[kk], a, preferred_element_type=F32)
        if conv:
            xr, xi = xk[0:n2], xk[n2:]
            gr, gi = g_ref[kk, 0], g_ref[kk, 1]
            yk = jnp.concatenate([xr * gr - xi * gi, xr * gi + xi * gr], axis=0).astype(BF16)
            bk = jnp.dot(u_ref[kk], yk, preferred_element_type=F32)
            o_ref[0, kk] = bk[0:n2].astype(o_ref.dtype)
            o_ref[1, kk] = bk[n2:].astype(o_ref.dtype)
        else:
            xk = xk / (asum_ref[...] + 1e-6)
            o_ref[kk, 0] = xk[0:n2]
            o_ref[kk, 1] = xk[n2:]


def _mid_spectrum(a5, tmat, asum):
    _, _, n1, n2, cw = a5.shape
    kb = _pick(n1, (MID_K1_PER_STEP,))
    return pl.pallas_call(
        functools.partial(_mid_kernel, conv=False),
        out_shape=jax.ShapeDtypeStruct((n1, 2, n2, cw), F32),
        grid=(n1 // kb,),
        in_specs=[pl.BlockSpec((None, 2, kb, n2, cw), lambda k: (0, 0, k, 0, 0)),
                  pl.BlockSpec((kb, 2 * n2, 2 * n2), lambda k: (k, 0, 0)),
                  pl.BlockSpec((1, cw), lambda k: (0, 0))],
        out_specs=pl.BlockSpec((kb, 2, n2, cw), lambda k: (k, 0, 0, 0)),
        compiler_params=_cparams(("parallel",)),
        name="hyena_spectrum",
    )(a5, tmat, asum)


def _mid_conv(a5, tmat, umat, gspec, order):
    nb, _, n1, n2, cw = a5.shape
    kb = _pick(n1, (MID_K1_PER_STEP,))
    return pl.pallas_call(
        functools.partial(_mid_kernel, conv=True),
        out_shape=jax.ShapeDtypeStruct((nb, 2, n1, n2, cw), BF16),
        grid=(n1 // kb, nb),
        in_specs=[pl.BlockSpec((None, 2, kb, n2, cw), lambda k, b: (b, 0, k, 0, 0)),
                  pl.BlockSpec((kb, 2 * n2, 2 * n2), lambda k, b: (k, 0, 0)),
                  pl.BlockSpec((kb, 2, n2, cw), lambda k, b: (k, 0, 0, order)),
                  pl.BlockSpec((kb, 2 * n2, 2 * n2), lambda k, b: (k, 0, 0))],
        out_specs=pl.BlockSpec((None, 2, kb, n2, cw), lambda k, b: (b, 0, k, 0, 0)),
        compiler_params=_cparams(("parallel", "arbitrary")),
        name="hyena_mid",
    )(a5, tmat, gspec, umat)


def _dft_tables(seq):
    n = 2 * seq
    n2 = DFT_N2
    n1 = n // n2
    two_pi = 2.0 * math.pi

    def cs(num, den):
        ang = (num % den).astype(F32) * (two_pi / den)
        return jnp.cos(ang), jnp.sin(ang)

    k1 = jnp.arange(n1, dtype=jnp.int32)
    c, s = cs(k1[:, None] * jnp.arange(n1, dtype=jnp.int32)[None, :], n1)
    f1_full = jnp.concatenate([c, -s], axis=0)
    i2 = jnp.arange(n2, dtype=jnp.int32)
    num = i2[None, None, :] * k1[:, None, None] + n1 * (i2[None, :, None] * i2[None, None, :])
    c, s = cs(num, n)
    tre, tim = c, -s
    tmat = jnp.concatenate([jnp.concatenate([tre, -tim], axis=2),
                            jnp.concatenate([tim, tre], axis=2)], axis=1)
    ure, uim = jnp.swapaxes(tre, 1, 2), -jnp.swapaxes(tim, 1, 2)
    umat = jnp.concatenate([jnp.concatenate([ure, -uim], axis=2),
                            jnp.concatenate([uim, ure], axis=2)], axis=1)
    c, s = cs(jnp.arange(n1 // 2, dtype=jnp.int32)[:, None] * k1[None, :], n1)
    fi = jnp.concatenate([c, -s], axis=1) * (1.0 / n)
    return (f1_full.astype(BF16), f1_full[:, :n1 // 2].astype(BF16), tmat.astype(BF16), umat.astype(BF16),
            fi.astype(BF16))


def _dft_tables_small(seq):
    n = 2 * seq
    k = jnp.arange(n, dtype=jnp.int32)
    ang = ((k[:, None] * k[None, :]) % n).astype(F32) * (2.0 * math.pi / n)
    c, s = jnp.cos(ang), jnp.sin(ang)
    fwd = jnp.concatenate([c, -s], axis=0)
    inv = jnp.concatenate([c[:seq], -s[:seq]], axis=1) * (1.0 / n)
    return fwd.astype(BF16), fwd[:, :seq].astype(BF16), inv.astype(BF16)


def _ctxconv_kernel(z_ref, xg_ref, taps_ref, asum_ref, bias_ref, ff_ref, fh_ref, fi_ref, o_ref):
    n = ff_ref.shape[1]
    z = z_ref[...]
    g = jnp.dot(ff_ref[...], taps_ref[...].astype(BF16), preferred_element_type=F32) / (asum_ref[...] + 1e-6)
    xk = jnp.dot(fh_ref[...], z, preferred_element_type=F32)
    xr, xi = xk[0:n], xk[n:]
    gr, gi = g[0:n], g[n:]
    yk = jnp.concatenate([xr * gr - xi * gi, xr * gi + xi * gr], axis=0).astype(BF16)
    y = jnp.dot(fi_ref[...], yk, preferred_element_type=F32)
    o_ref[...] = (xg_ref[...].astype(F32) * (y + bias_ref[...] * z.astype(F32))).astype(o_ref.dtype)


def _ctx_longconv(z, xg, taps, asum, bias, tabs, order, *, bsz, seq):
    ff, fh, fi = tabs
    n = 2 * seq
    cw = BRANCH_W
    return pl.pallas_call(
        _ctxconv_kernel,
        out_shape=jax.ShapeDtypeStruct((bsz * seq, cw), BF16),
        grid=(bsz,),
        in_specs=[pl.BlockSpec((seq, cw), lambda b: (b, 0)),
                  pl.BlockSpec((seq, cw), lambda b: (b, 0)),
                  pl.BlockSpec((n, cw), lambda b: (0, order)),
                  pl.BlockSpec((1, cw), lambda b: (0, order)),
                  pl.BlockSpec((1, cw), lambda b: (0, 0)),
                  pl.BlockSpec((2 * n, n), lambda b: (0, 0)),
                  pl.BlockSpec((2 * n, seq), lambda b: (0, 0)),
                  pl.BlockSpec((seq, 2 * n), lambda b: (0, 0))],
        out_specs=pl.BlockSpec((seq, cw), lambda b: (b, 0)),
        compiler_params=_cparams(("parallel",)),
        name="hyena_ctx_conv",
    )(z, xg, taps, asum, bias, ff, fh, fi)


def _hyena_long(v0, x1, x2, taps, asum, hy_bias, tabs, *, bsz, seq):
    f1_full, f1_half, tmat, umat, fi = tabs
    n2 = DFT_N2
    n1 = 2 * seq // n2
    cw = BRANCH_W
    tn = 4096
    at = _lmm(f1_full, taps.reshape(1, n1, n2 * 2 * cw), tn=tn, out_dtype=BF16, name="taps_dft1")
    gspec = _mid_spectrum(at.reshape(1, 2, n1, n2, 2 * cw), tmat, asum)

    def conv(z, xg, order):
        z2 = z.reshape(bsz, n1 // 2, n2 * cw)
        a = _lmm(f1_half, z2, tn=tn, out_dtype=BF16, name="hyena_dft1")
        bk = _mid_conv(a.reshape(bsz, 2, n1, n2, cw), tmat, umat, gspec, order)
        bias_t = jnp.tile(hy_bias[order].reshape(1, cw), (1, tn // cw))
        y = _lmm(fi, bk.reshape(bsz, 2 * n1, n2 * cw), tn=tn, out_dtype=BF16,
                 gate_args=(xg.reshape(bsz, n1 // 2, n2 * cw), z2, bias_t), name="hyena_idft2")
        return y.reshape(bsz * seq, cw)

    return conv(conv(v0, x1, 0), x2, 1)


def _s5_kernel(u_ref, bre_ref, bim_ref, cre_ref, cim_ref, are_ref, aim_ref, s0_ref,
               y_ref, sfin_ref,
               uf, lhs, bur, bui, ybuf, pw_r, pw_i, car_r, car_i, st_r, st_i, *, tseg, rev):
    i = pl.program_id(1)
    sw = S5_SW
    gw = sw // S5_SUPER
    a_re = are_ref[...]
    a_im = aim_ref[...]

    @pl.when(jnp.logical_and(pl.program_id(0) == 0, i == 0))
    def _():
        def pbody(r, carry):
            cr, ci = carry
            pw_r[pl.ds(r, 1), :] = cr
            pw_i[pl.ds(r, 1), :] = ci
            return cr * a_re - ci * a_im, cr * a_im + ci * a_re
        lax.fori_loop(0, tseg, pbody, (a_re, a_im))

    @pl.when(i == 0)
    def _():
        st_r[...] = s0_ref[:, 0:sw]
        st_i[...] = s0_ref[:, sw:]

    for sg in range(S5_SUPER):
        uf[sg] = u_ref[:, sg * 128:(sg + 1) * 128].astype(F32)

    def gather(r, _):
        for sg in range(S5_SUPER):
            lhs[sg, pl.ds(pl.multiple_of(r * 8, 8), 8), :] = uf[sg, pl.ds(r, 8, stride=tseg), :]
        return 0
    lax.fori_loop(0, tseg, gather, 0, unroll=S5_UNROLL)

    for sg in range(S5_SUPER):
        lb = lhs[sg].astype(BF16)
        bur[:, sg * gw:(sg + 1) * gw] = jnp.dot(lb, bre_ref[sg], preferred_element_type=F32)
        bui[:, sg * gw:(sg + 1) * gw] = jnp.dot(lb, bim_ref[sg], preferred_element_type=F32)

    for sg in range(S5_SUPER):
        cols = slice(sg * gw, (sg + 1) * gw)
        ar = jnp.broadcast_to(a_re[:, cols], (8, gw))
        ai = jnp.broadcast_to(a_im[:, cols], (8, gw))

        def sbody(k, carry, cols=cols, ar=ar, ai=ai):
            sr, si = carry
            r = (tseg - 1 - k) if rev else k
            rows = pl.ds(pl.multiple_of(r * 8, 8), 8)
            nr = ar * sr - ai * si + bur[rows, cols]
            ni = ar * si + ai * sr + bui[rows, cols]
            bur[rows, cols] = nr
            bui[rows, cols] = ni
            return nr, ni
        zero = jnp.zeros((8, gw), F32)
        lax.fori_loop(0, tseg, sbody, (zero, zero))

    at_r = pw_r[tseg - 1:tseg, :]
    at_i = pw_i[tseg - 1:tseg, :]
    end_row = 0 if rev else (tseg - 1) * 8
    cr = st_r[...]
    ci = st_i[...]
    order = range(7, -1, -1) if rev else range(8)
    for s in order:
        car_r[s:s + 1, :] = cr
        car_i[s:s + 1, :] = ci
        er = bur[end_row + s:end_row + s + 1, :]
        ei = bui[end_row + s:end_row + s + 1, :]
        cr, ci = er + at_r * cr - at_i * ci, ei + at_r * ci + at_i * cr
    st_r[...] = cr
    st_i[...] = ci
    sfin_ref[:, 0:sw] = cr
    sfin_ref[:, sw:] = ci

    for sg in range(S5_SUPER):
        cols = slice(sg * gw, (sg + 1) * gw)
        kr = car_r[:, cols]
        kim = car_i[:, cols]

        def fbody(r, _, cols=cols, kr=kr, kim=kim):
            pidx = (tseg - 1 - r) if rev else r
            pr = pw_r[pl.ds(pidx, 1), cols]
            pi_ = pw_i[pl.ds(pidx, 1), cols]
            rows = pl.ds(pl.multiple_of(r * 8, 8), 8)
            bur[rows, cols] = bur[rows, cols] + (pr * kr - pi_ * kim)
            bui[rows, cols] = bui[rows, cols] + (pr * kim + pi_ * kr)
            return 0
        lax.fori_loop(0, tseg, fbody, 0, unroll=S5_UNROLL)

    for sg in range(S5_SUPER):
        cols = slice(sg * gw, (sg + 1) * gw)
        ybuf[sg] = (jnp.dot(bur[:, cols].astype(BF16), cre_ref[sg], preferred_element_type=F32)
                    + jnp.dot(bui[:, cols].astype(BF16), cim_ref[sg], preferred_element_type=F32))

    def scatter(r, _):
        for sg in range(S5_SUPER):
            y_ref[sg, pl.ds(r, 8, stride=tseg), :] = ybuf[sg, pl.ds(pl.multiple_of(r * 8, 8), 8), :]
        return 0
    lax.fori_loop(0, tseg, scatter, 0, unroll=S5_UNROLL)


def _s5_scan(proj, s5p, s0, *, bsz, seq, tseg, rev):
    bre, bim, cre, cim, are, aim = s5p
    tr = 8 * tseg
    nt = seq // tr
    sw = S5_SW
    gw = sw // S5_SUPER
    tile = (lambda b, i: (b * nt + (nt - 1 - i), U_U)) if rev else (lambda b, i: (b * nt + i, U_U))
    otile = (lambda b, i: (0, b * nt + (nt - 1 - i), 0)) if rev else (lambda b, i: (0, b * nt + i, 0))
    full = lambda shape: pl.BlockSpec(shape, lambda b, i: tuple(0 for _ in shape))
    return pl.pallas_call(
        functools.partial(_s5_kernel, tseg=tseg, rev=rev),
        out_shape=(jax.ShapeDtypeStruct((S5_SUPER, bsz * seq, 128), F32),
                   jax.ShapeDtypeStruct((bsz, 1, 2 * sw), F32)),
        grid=(bsz, nt),
        in_specs=[pl.BlockSpec((tr, UNIT), tile),
                  full((S5_SUPER, 128, gw)), full((S5_SUPER, 128, gw)),
                  full((S5_SUPER, gw, 128)), full((S5_SUPER, gw, 128)),
                  full((1, sw)), full((1, sw)),
                  pl.BlockSpec((None, 1, 2 * sw), lambda b, i: (b, 0, 0))],
        out_specs=(pl.BlockSpec((S5_SUPER, tr, 128), otile),
                   pl.BlockSpec((None, 1, 2 * sw), lambda b, i: (b, 0, 0))),
        scratch_shapes=[pltpu.VMEM((S5_SUPER, tr, 128), F32), pltpu.VMEM((S5_SUPER, tr, 128), F32),
                        pltpu.VMEM((tr, sw), F32), pltpu.VMEM((tr, sw), F32),
                        pltpu.VMEM((S5_SUPER, tr, 128), F32),
                        pltpu.VMEM((tseg, sw), F32), pltpu.VMEM((tseg, sw), F32),
                        pltpu.VMEM((8, sw), F32), pltpu.VMEM((8, sw), F32),
                        pltpu.VMEM((1, sw), F32), pltpu.VMEM((1, sw), F32)],
        compiler_params=_cparams(("arbitrary", "arbitrary")),
        name="s5_bwd" if rev else "s5_fwd",
    )(proj, bre, bim, cre, cim, are, aim, s0)


def _s5_params(a_re, a_im, log_step, b_re, b_im, c_re, c_im):
    g, p, ci = S5_GROUPS, S5_STATE, S5_GROUP
    dt = jnp.exp(log_step)[:, None]
    mag = jnp.exp(a_re * dt)
    ar, ai = mag * jnp.cos(a_im * dt), mag * jnp.sin(a_im * dt)
    den = a_re * a_re + a_im * a_im
    fr = ((ar - 1.0) * a_re + ai * a_im) / den
    fi = (ai * a_re - (ar - 1.0) * a_im) / den
    bbr = fr[..., None] * b_re - fi[..., None] * b_im
    bbi = fr[..., None] * b_im + fi[..., None] * b_re
    eye = jnp.eye(8, dtype=F32)

    def blockdiag_b(m):
        m4 = m.reshape(S5_SUPER, 8, p, ci)
        return jnp.einsum('sgpc,gh->sgchp', m4, eye).reshape(S5_SUPER, 8 * ci, 8 * p).astype(BF16)

    def blockdiag_c(m):
        m4 = m.reshape(S5_SUPER, 8, ci, p)
        return jnp.einsum('sgcp,gh->sgphc', m4, eye).reshape(S5_SUPER, 8 * p, 8 * ci).astype(BF16)

    return (blockdiag_b(bbr), blockdiag_b(bbi), blockdiag_c(c_re), blockdiag_c(-c_im),
            ar.reshape(1, g * p), ai.reshape(1, g * p))


def _merge_kernel(ya_ref, yh_ref, hg_ref, yatt_ref, attg_ref, ysf_ref, ysb_ref, u_ref, s5g_ref,
                  mg0_ref, mg1_ref, mg2_ref, mg3_ref, x_ref, mod_ref, s5d_ref, wglu_ref, wb_ref, wout_ref,
                  lng_ref, lnb_ref, o_ref, *, alpha):
    f = lambda r: r[...].astype(F32)
    y_a = f(ya_ref)
    y_h = f(yh_ref) * _silu(f(hg_ref))
    y_c = f(yatt_ref) * _silu(f(attg_ref))
    ys = jnp.concatenate([ysf_ref[sg] + ysb_ref[sg] for sg in range(S5_SUPER)], axis=1)
    y = ys + s5d_ref[...] * f(u_ref)
    zg = jax.nn.gelu(y)
    glu = jnp.dot(zg.astype(BF16), wglu_ref[...], preferred_element_type=F32)
    y_d = zg * _sigmoid(glu) * _silu(f(s5g_ref))
    mix = None
    for n, (yn, mg) in enumerate(((y_a, mg0_ref), (y_h, mg1_ref), (y_c, mg2_ref), (y_d, mg3_ref))):
        term = _sigmoid(f(mg)) * jnp.dot(yn.astype(BF16), wb_ref[n], preferred_element_type=F32)
        mix = term if mix is None else mix + term
    out = jnp.dot(mix.astype(BF16), wout_ref[...], preferred_element_type=F32)
    gate = mod_ref[2:3, :]
    r = alpha * x_ref[...] + gate * out
    mu = jnp.mean(r, axis=1, keepdims=True)
    rc = r - mu
    var = jnp.mean(rc * rc, axis=1, keepdims=True)
    o_ref[...] = rc * lax.rsqrt(var + LN_EPS) * lng_ref[...] + lnb_ref[...]


def _merge(ya, yh, yatt, ysf, ysb, proj, x2d, mod3, s5d, wglu, wb, wout, lng, lnb, *, seq, tm, alpha):
    m = x2d.shape[0]
    nt_seq = seq // tm
    nb = mod3.shape[0]
    mod_map = (lambda i: (i // nt_seq, 0, 0)) if nb > 1 else (lambda i: (0, 0, 0))
    row = lambda w_: pl.BlockSpec((tm, w_), lambda i: (i, 0))
    pcol = lambda unit: pl.BlockSpec((tm, UNIT), lambda i: (i, unit))
    mcol = lambda n: pl.BlockSpec((tm, 2 * UNIT), lambda i: (i, U_MERGE // 2 + n))
    full = lambda shape: pl.BlockSpec(shape, lambda i: tuple(0 for _ in shape))
    w_ = BRANCH_W
    s5row = pl.BlockSpec((S5_SUPER, tm, 128), lambda i: (0, i, 0))
    return pl.pallas_call(
        functools.partial(_merge_kernel, alpha=alpha),
        out_shape=jax.ShapeDtypeStruct((m, D_MODEL), F32),
        grid=(m // tm,),
        in_specs=[row(w_), row(w_), pcol(U_HG), row(w_), pcol(U_ATTG), s5row, s5row, pcol(U_U), pcol(U_S5G),
                  mcol(0), mcol(1), mcol(2), mcol(3), row(D_MODEL),
                  pl.BlockSpec((None, 3, D_MODEL), mod_map),
                  full((1, w_)), full((w_, w_)), full((N_BRANCH, w_, D_MODEL)), full((D_MODEL, D_MODEL)),
                  full((1, D_MODEL)), full((1, D_MODEL))],
        out_specs=row(D_MODEL),
        compiler_params=_cparams(("parallel",)),
        name="merge_out_norm",
    )(ya, yh, proj, yatt, proj, ysf, ysb, proj, proj, proj, proj, proj, proj, x2d, mod3,
      s5d, wglu, wb, wout, lng, lnb)


def _rope_tables(n_lat):
    rows = n_lat // GRID_W
    row = jnp.broadcast_to(jnp.arange(rows)[:, None], (rows, GRID_W)).reshape(-1)
    col = jnp.broadcast_to(jnp.arange(GRID_W)[None, :], (rows, GRID_W)).reshape(-1)
    half = HEAD_DIM // 2
    inv = 1.0 / (ROPE_BASE ** (jnp.arange(0, half, 2, dtype=F32) / half))
    ar, ac = row[:, None] * inv, col[:, None] * inv
    cos64 = jnp.concatenate([jnp.cos(ar), jnp.cos(ar), jnp.cos(ac), jnp.cos(ac)], axis=-1)
    sin64 = jnp.concatenate([-jnp.sin(ar), jnp.sin(ar), -jnp.sin(ac), jnp.sin(ac)], axis=-1)
    return jnp.tile(cos64, (1, 2)), jnp.tile(sin64, (1, 2))


def _pick(n, prefs):
    for t in prefs:
        if n % t == 0:
            return t
    return n


def kernel(x, c, ctx, c_ctx, w_mod, b_mod, w_in, conv_a, conv_h, hy_w1, hy_b1, hy_w2, hy_b2, hy_w3, hy_freq,
           hy_delta, hy_bias, lam_q1, lam_k1, lam_q2, lam_k2, attn_norm_g, s5_a_re, s5_a_im, s5_log_step,
           s5_b_re, s5_b_im, s5_c_re, s5_c_im, s5_d, s5_w_glu, w_branch, w_out, ln_g, ln_b):
    bsz, seq, d = x.shape
    seq_c = ctx.shape[1]
    depth = w_in.shape[0]
    alpha = (2.0 * depth) ** 0.25
    assert d == D_MODEL and seq % 1024 == 0 and seq_c % 256 == 0 and bsz + 1 <= 8

    cos_t, sin_t = _rope_tables(seq)
    dft_lat = _dft_tables(seq)
    dft_ctx = _dft_tables_small(seq_c)
    cvec = jnp.zeros((8, d), F32).at[0:bsz].set(c).at[bsz].set(c_ctx)
    tm_in = _pick(seq, (2048, 1024))
    tm_el = _pick(seq, (512,))
    tq = _pick(seq, (256,))
    ck = _pick(seq, (1024, 512))
    tseg_lat = 128
    tseg_ctx = seq_c // 8

    x2 = x.reshape(bsz * seq, d)
    xc2 = ctx.reshape(bsz * seq_c, d)
    for l in range(depth):
        last = l == depth - 1
        lam_init = 0.8 - 0.6 * math.exp(-0.3 * l)
        mod = _modulation(cvec, w_mod, b_mod, l).reshape(8, 3, d)
        mod_lat, mod_ctx = mod[0:bsz], mod[bsz:bsz + 1]
        w_l = w_in[l].reshape(d, N_UNITS, UNIT)[:, jnp.array(UNIT_PERM)].reshape(d, PROJ_W).astype(BF16)
        lamp = jnp.stack([lam_q1[l], lam_k1[l], lam_q2[l], lam_k2[l]], axis=0)
        g_att = attn_norm_g[l].reshape(1, V_DIM)
        s5f = _s5_params(s5_a_re[l, 0], s5_a_im[l, 0], s5_log_step[l, 0], s5_b_re[l, 0], s5_b_im[l, 0],
                         s5_c_re[l, 0], s5_c_im[l, 0])
        s5b = _s5_params(s5_a_re[l, 1], s5_a_im[l, 1], s5_log_step[l, 1], s5_b_re[l, 1], s5_b_im[l, 1],
                         s5_c_re[l, 1], s5_c_im[l, 1])
        hp = (hy_w1[l], hy_b1[l], hy_w2[l], hy_b2[l], hy_w3[l], hy_freq[l], hy_delta[l])
        wglu = s5_w_glu[l].astype(BF16)
        wb = w_branch[l].astype(BF16)
        wout = w_out[l].astype(BF16)
        s5d = s5_d[l].reshape(1, BRANCH_W)
        lng, lnb = ln_g[l].reshape(1, d), ln_b[l].reshape(1, d)

        ncols_c = 3 * 1024 if last else PROJ_W
        projc = _inproj(xc2, mod_ctx, w_l[:, :ncols_c], cos_t, sin_t, seq=seq_c, rope=False,
                        tm=bsz * seq_c, ncols=ncols_c)
        zero_state = jnp.zeros((bsz, 1, 2 * S5_SW), F32)
        ycf, scf = _s5_scan(projc, s5f, zero_state, bsz=bsz, seq=seq_c, tseg=tseg_ctx, rev=False)
        ycb, scb = _s5_scan(projc, s5b, zero_state, bsz=bsz, seq=seq_c, tseg=tseg_ctx, rev=True)

        proj = _inproj(x2, mod_lat, w_l, cos_t, sin_t, seq=seq, rope=True, tm=tm_in, ncols=PROJ_W)
        yatt = _attention_lat(proj, projc, lamp, g_att, bsz=bsz, seq=seq, seq_c=seq_c,
                              lam_init=lam_init, tq=tq, nsub=2, ck=ck)
        ysf, _ = _s5_scan(proj, s5f, scf, bsz=bsz, seq=seq, tseg=tseg_lat, rev=False)
        ysb, _ = _s5_scan(proj, s5b, scb, bsz=bsz, seq=seq, tseg=tseg_lat, rev=True)
        ya, v0, x1, x2h = _convgate(proj, conv_a[l], conv_h[l], bsz=bsz, seq=seq, tm=tm_el)
        taps, asum = _hyena_taps(hp, seq=seq, tr=512)
        yh = _hyena_long(v0, x1, x2h, taps, asum, hy_bias[l], dft_lat, bsz=bsz, seq=seq)
        x_new = _merge(ya, yh, yatt, ysf, ysb, proj, x2, mod_lat, s5d, wglu, wb, wout, lng, lnb,
                       seq=seq, tm=tm_el, alpha=alpha)

        if not last:
            yatt_c = _attention_ctx(projc, lamp, g_att, bsz=bsz, seq_c=seq_c, lam_init=lam_init)
            ya_c, v0c, x1c, x2c = _convgate(projc, conv_a[l], conv_h[l], bsz=bsz, seq=seq_c, tm=seq_c)
            taps_c, asum_c = _hyena_taps(hp, seq=seq_c, tr=seq_c)
            b0 = hy_bias[l, 0].reshape(1, BRANCH_W)
            b1 = hy_bias[l, 1].reshape(1, BRANCH_W)
            z1c = _ctx_longconv(v0c, x1c, taps_c, asum_c, b0, dft_ctx, 0, bsz=bsz, seq=seq_c)
            yh_c = _ctx_longconv(z1c, x2c, taps_c, asum_c, b1, dft_ctx, 1, bsz=bsz, seq=seq_c)
            xc2 = _merge(ya_c, yh_c, yatt_c, ycf, ycb, projc, xc2, mod_ctx, s5d, wglu, wb, wout, lng, lnb,
                         seq=seq_c, tm=seq_c, alpha=alpha)
        x2 = x_new
    return x2.reshape(bsz, seq, d)
```

```python
import functools
import math

import jax
import jax.numpy as jnp
from jax import lax
from jax.experimental import pallas as pl
from jax.experimental.pallas import tpu as pltpu

F32 = jnp.float32
BF16 = jnp.bfloat16
HIGHEST = lax.Precision.HIGHEST

D_MODEL = 1024
BRANCH_W = 512
N_HEADS = 4
HEAD_DIM = 64
V_DIM = 128
GRID_W = 64
ROPE_BASE = 10000.0
HYENA_BANDS = 16
HYENA_HIDDEN = 64
HYENA_SHIFT = 0.05
S5_GROUP = 16
S5_GROUPS = 32
S5_STATE = 64
S5_SUPER = 4
S5_SW = S5_GROUPS * S5_STATE
LN_EPS = 1e-5
N_BRANCH = 4
UNIT = 512
N_UNITS = 22
PROJ_W = N_UNITS * UNIT
UNIT_PERM = (0, 1, 11, 12, 2, 4, 5, 7, 8, 9, 3, 6, 13, 10, 14, 15, 16, 17, 18, 19, 20, 21)
U_K, U_V, U_Q, U_ATTG, U_U, U_CONV, U_AB, U_S5G, U_HG, U_MERGE = 0, 1, 2, 3, 4, 5, 10, 12, 13, 14
DFT_N2 = 128
MID_K1_PER_STEP = 8
S5_UNROLL = 8
ATTN_TQ = 256
ATTN_SUBTILES = 2
VMEM_LIMIT = 56 * 1024 * 1024


def _cparams(sem):
    return pltpu.CompilerParams(dimension_semantics=sem, vmem_limit_bytes=VMEM_LIMIT)


def _sigmoid(v):
    return 0.5 * jnp.tanh(0.5 * v) + 0.5


def _silu(v):
    return v * _sigmoid(v)


def _mod_kernel(s_ref, w_ref, b_ref, o_ref):
    s = _silu(s_ref[...])
    o_ref[...] = jnp.dot(s, w_ref[...], preferred_element_type=F32, precision=HIGHEST) + b_ref[...]


def _modulation(cvec, w_mod, b_mod, layer):
    depth, _, n = w_mod.shape
    tn = 512
    return pl.pallas_call(
        _mod_kernel,
        out_shape=jax.ShapeDtypeStruct((8, n), F32),
        grid=(n // tn,),
        in_specs=[pl.BlockSpec((8, D_MODEL), lambda j: (0, 0)),
                  pl.BlockSpec((None, D_MODEL, tn), lambda j: (layer, 0, j)),
                  pl.BlockSpec((None, 1, tn), lambda j: (layer, 0, j))],
        out_specs=pl.BlockSpec((8, tn), lambda j: (0, j)),
        compiler_params=_cparams(("parallel",)),
        name="modulation",
    )(cvec, w_mod, b_mod.reshape(depth, 1, n))


def _inproj_kernel(x_ref, mod_ref, w_ref, cos_ref, sin_ref, o_ref, h_ref, *, rope):
    j = pl.program_id(1)

    @pl.when(j == 0)
    def _():
        shift = mod_ref[0:1, :]
        scale = mod_ref[1:2, :]
        h_ref[...] = (x_ref[...] * (1.0 + scale) + shift).astype(BF16)

    def project():
        return jnp.dot(h_ref[...], w_ref[...], preferred_element_type=F32)

    if not rope:
        o_ref[...] = project().astype(o_ref.dtype)
        return

    @pl.when(j < 2)
    def _():
        acc = project()
        cs = cos_ref[...]
        sn = sin_ref[...]
        lane = lax.broadcasted_iota(jnp.int32, cs.shape, 1)
        first = (lane % 32) < 16
        for cb in range(UNIT // 128):
            t = acc[:, cb * 128:(cb + 1) * 128]
            partner = jnp.where(first, pltpu.roll(t, 128 - 16, 1), pltpu.roll(t, 16, 1))
            o_ref[:, cb * 128:(cb + 1) * 128] = (t * cs + partner * sn).astype(o_ref.dtype)
        o_ref[:, UNIT:] = acc[:, UNIT:].astype(o_ref.dtype)

    @pl.when(j >= 2)
    def _():
        o_ref[...] = project().astype(o_ref.dtype)


def _inproj(x2d, mod3, w, cos_t, sin_t, *, seq, rope, tm, ncols):
    m = x2d.shape[0]
    tn = 1024
    nt_seq = max(seq // tm, 1)
    nb = mod3.shape[0]
    mod_map = (lambda i, j: (i // nt_seq, 0, 0)) if nb > 1 else (lambda i, j: (0, 0, 0))
    return pl.pallas_call(
        functools.partial(_inproj_kernel, rope=rope),
        out_shape=jax.ShapeDtypeStruct((m, ncols), BF16),
        grid=(m // tm, ncols // tn),
        in_specs=[pl.BlockSpec((tm, D_MODEL), lambda i, j: (i, 0)),
                  pl.BlockSpec((None, 3, D_MODEL), mod_map),
                  pl.BlockSpec((D_MODEL, tn), lambda i, j: (0, j)),
                  pl.BlockSpec((tm, 128), lambda i, j: (i % nt_seq, 0)),
                  pl.BlockSpec((tm, 128), lambda i, j: (i % nt_seq, 0))],
        out_specs=pl.BlockSpec((tm, tn), lambda i, j: (i, j)),
        scratch_shapes=[pltpu.VMEM((tm, D_MODEL), BF16)],
        compiler_params=_cparams(("parallel", "arbitrary")),
        name="inproj_rope" if rope else "inproj_ctx",
    )(x2d, mod3, w, cos_t, sin_t)


def _head_output(o1, o2, lamp_ref, g_ref, lam_init):
    lp = lamp_ref[...]
    lam = (jnp.exp(jnp.sum(lp[0:1] * lp[1:2], axis=1, keepdims=True))
           - jnp.exp(jnp.sum(lp[2:3] * lp[3:4], axis=1, keepdims=True)) + lam_init)
    od = o1 - lam * o2
    return od * lax.rsqrt(jnp.mean(od * od, axis=1, keepdims=True) + 1e-5) * g_ref[...] * (1.0 - lam_init)


def _attn_ctx_kernel(q_ref, kc_ref, vc_ref, lamp_ref, g_ref, o_ref, *, tq, lam_init):
    q = q_ref[...].astype(F32) * (HEAD_DIM ** -0.5)
    lane = lax.broadcasted_iota(jnp.int32, q.shape, 1)
    qz = jnp.concatenate([jnp.where(lane < HEAD_DIM, q, 0.0), jnp.where(lane >= HEAD_DIM, q, 0.0)],
                         axis=0).astype(BF16)
    s = lax.dot_general(qz, kc_ref[...], (((1,), (1,)), ((), ())), preferred_element_type=F32)
    p = jnp.exp(s - jnp.max(s, axis=1, keepdims=True))
    l = jnp.sum(p, axis=1, keepdims=True)
    o = jnp.dot(p.astype(BF16), vc_ref[...], preferred_element_type=F32) / l
    o_ref[...] = _head_output(o[0:tq], o[tq:], lamp_ref, g_ref, lam_init).astype(o_ref.dtype)


def _attention_ctx(proj_c, lamp, g, *, bsz, seq_c, lam_init):
    hq = U_Q * UNIT // 128
    hk = U_K * UNIT // 128
    hv = U_V * UNIT // 128
    return pl.pallas_call(
        functools.partial(_attn_ctx_kernel, tq=seq_c, lam_init=lam_init),
        out_shape=jax.ShapeDtypeStruct((bsz * seq_c, N_HEADS * V_DIM), BF16),
        grid=(bsz, N_HEADS),
        in_specs=[pl.BlockSpec((seq_c, 128), lambda b, h: (b, hq + h)),
                  pl.BlockSpec((seq_c, 128), lambda b, h: (b, hk + h)),
                  pl.BlockSpec((seq_c, 128), lambda b, h: (b, hv + h)),
                  pl.BlockSpec((4, HEAD_DIM), lambda b, h: (0, 0)),
                  pl.BlockSpec((1, V_DIM), lambda b, h: (0, 0))],
        out_specs=pl.BlockSpec((seq_c, V_DIM), lambda b, h: (b, h)),
        compiler_params=_cparams(("parallel", "parallel")),
        name="diff_attn_ctx",
    )(proj_c, proj_c, proj_c, lamp, g)


def _attn_lat_kernel(q_ref, k_ref, v_ref, kc_ref, vc_ref, lamp_ref, g_ref, o_ref,
                     qz, vext, vcext, sbuf, pbuf, abuf, m_s, acc, *, tq, nsub, ck, n_chunks, lam_init):
    @pl.when(pl.program_id(2) == 0)
    def _():
        vext[:, 0:V_DIM] = v_ref[...]
        vext[:, V_DIM:] = jnp.ones((vext.shape[0], V_DIM), BF16)
        vcext[:, 0:V_DIM] = vc_ref[...]
        vcext[:, V_DIM:] = jnp.ones((vcext.shape[0], V_DIM), BF16)

    subs = range(nsub)
    for h in subs:
        q = q_ref[h * tq:(h + 1) * tq, :].astype(F32) * (HEAD_DIM ** -0.5 * math.log2(math.e))
        lane = lax.broadcasted_iota(jnp.int32, q.shape, 1)
        qz[h, 0:tq, :] = jnp.where(lane < HEAD_DIM, q, 0.0).astype(BF16)
        qz[h, tq:, :] = jnp.where(lane >= HEAD_DIM, q, 0.0).astype(BF16)

    def qk(h, kblk):
        return lax.dot_general(qz[h], kblk, (((1,), (1,)), ((), ())), preferred_element_type=F32)

    def kchunk(c):
        return k_ref[pl.ds(pl.multiple_of(c * ck, ck), ck), :]

    def vchunk(c):
        return vext[pl.ds(pl.multiple_of(c * ck, ck), ck), :]

    def score(slot, c):
        kblk = kchunk(c)
        for h in subs:
            sbuf[h, slot] = qk(h, kblk)

    def softmax(slot):
        for h in subs:
            s = sbuf[h, slot]
            m_prev = m_s[h]
            m_new = jnp.maximum(m_prev, jnp.max(s, axis=1, keepdims=True))
            abuf[h, slot] = jnp.exp2(m_prev - m_new)
            pbuf[h, slot] = jnp.exp2(s - m_new).astype(BF16)
            m_s[h] = m_new

    def pv(slot, c):
        vblk = vchunk(c)
        for h in subs:
            acc[h] = abuf[h, slot] * acc[h] + jnp.dot(pbuf[h, slot], vblk, preferred_element_type=F32)

    for h in subs:
        s = qk(h, kc_ref[...])
        m0 = jnp.max(s, axis=1, keepdims=True)
        m_s[h] = m0
        acc[h] = jnp.dot(jnp.exp2(s - m0).astype(BF16), vcext[...], preferred_element_type=F32)

    score(0, 0)
    softmax(0)
    score(1, 1)

    def body(j, _):
        a = 2 * j
        pv(0, a)
        score(0, a + 2)
        softmax(1)
        pv(1, a + 1)
        score(1, a + 3)
        softmax(0)
        return 0
    lax.fori_loop(0, n_chunks // 2 - 1, body, 0)

    pv(0, n_chunks - 2)
    softmax(1)
    pv(1, n_chunks - 1)

    for h in subs:
        a_ = acc[h]
        o = a_[:, 0:V_DIM] / a_[:, V_DIM:]
        o_ref[h * tq:(h + 1) * tq, :] = _head_output(o[0:tq], o[tq:], lamp_ref, g_ref,
                                                     lam_init).astype(o_ref.dtype)


def _attention_lat(proj, proj_c, lamp, g, *, bsz, seq, seq_c, lam_init, tq, nsub, ck):
    tqs = tq * nsub
    nq = seq // tqs
    n_chunks = seq // ck
    assert n_chunks % 2 == 0 and n_chunks >= 2
    hq = U_Q * UNIT // 128
    hk = U_K * UNIT // 128
    hv = U_V * UNIT // 128
    return pl.pallas_call(
        functools.partial(_attn_lat_kernel, tq=tq, nsub=nsub, ck=ck, n_chunks=n_chunks, lam_init=lam_init),
        out_shape=jax.ShapeDtypeStruct((bsz * seq, N_HEADS * V_DIM), BF16),
        grid=(bsz, N_HEADS, nq),
        in_specs=[pl.BlockSpec((tqs, 128), lambda b, h, qi: (b * nq + qi, hq + h)),
                  pl.BlockSpec((seq, 128), lambda b, h, qi: (b, hk + h)),
                  pl.BlockSpec((seq, 128), lambda b, h, qi: (b, hv + h)),
                  pl.BlockSpec((seq_c, 128), lambda b, h, qi: (b, hk + h)),
                  pl.BlockSpec((seq_c, 128), lambda b, h, qi: (b, hv + h)),
                  pl.BlockSpec((4, HEAD_DIM), lambda b, h, qi: (0, 0)),
                  pl.BlockSpec((1, V_DIM), lambda b, h, qi: (0, 0))],
        out_specs=pl.BlockSpec((tqs, V_DIM), lambda b, h, qi: (b * nq + qi, h)),
        scratch_shapes=[pltpu.VMEM((nsub, 2 * tq, 128), BF16),
                        pltpu.VMEM((seq, 2 * V_DIM), BF16), pltpu.VMEM((seq_c, 2 * V_DIM), BF16),
                        pltpu.VMEM((nsub, 2, 2 * tq, ck), F32), pltpu.VMEM((nsub, 2, 2 * tq, ck), BF16),
                        pltpu.VMEM((nsub, 2, 2 * tq, 1), F32), pltpu.VMEM((nsub, 2 * tq, 1), F32),
                        pltpu.VMEM((nsub, 2 * tq, 2 * V_DIM), F32)],
        compiler_params=_cparams(("parallel", "parallel", "arbitrary")),
        name="diff_attn",
    )(proj, proj, proj, proj_c, proj_c, lamp, g)


def _conv3(cur, prv, nxt, w):
    tm = cur.shape[0]
    row = lax.broadcasted_iota(jnp.int32, cur.shape, 0)
    dn = jnp.where(row == 0, prv, pltpu.roll(cur, 1, 0))
    up = jnp.where(row == tm - 1, nxt, pltpu.roll(cur, tm - 1, 0))
    return dn * w[0:1] + cur * w[1:2] + up * w[2:3]


def _convgate_kernel(main_ref, prev_ref, next_ref, ab_ref, ca_ref, ch_ref,
                     ya_ref, v0_ref, x1_ref, x2_ref, *, nt):
    i = pl.program_id(1)
    w_ = BRANCH_W
    m = main_ref[...].astype(F32)
    pv = jnp.where(i > 0, prev_ref[15:16, :].astype(F32), 0.0)
    nx = jnp.where(i < nt - 1, next_ref[0:1, :].astype(F32), 0.0)
    ab = ab_ref[...].astype(F32)
    p = m[:, 0:w_] * m[:, w_:2 * w_]
    p_prev = pv[:, 0:w_] * pv[:, w_:2 * w_]
    p_next = nx[:, 0:w_] * nx[:, w_:2 * w_]
    ya = ab[:, 0:w_] * _conv3(p, p_prev, p_next, ca_ref[...]) * _silu(ab[:, w_:])
    ya_ref[...] = ya.astype(ya_ref.dtype)
    ch = ch_ref[...]
    for n, ref in enumerate((v0_ref, x1_ref, x2_ref)):
        lo, hi = (2 + n) * w_, (3 + n) * w_
        ref[...] = _conv3(m[:, lo:hi], pv[:, lo:hi], nx[:, lo:hi], ch[:, n * w_:(n + 1) * w_]).astype(ref.dtype)


def _convgate(proj, conv_a, conv_h, *, bsz, seq, tm):
    m = bsz * seq
    nt = seq // tm
    cw = 5 * UNIT
    hb = 16
    nhb = m // hb
    out = jax.ShapeDtypeStruct((m, BRANCH_W), BF16)
    ospec = pl.BlockSpec((tm, BRANCH_W), lambda b, i: (b * nt + i, 0))
    return pl.pallas_call(
        functools.partial(_convgate_kernel, nt=nt),
        out_shape=(out, out, out, out),
        grid=(bsz, nt),
        in_specs=[pl.BlockSpec((tm, cw), lambda b, i: (b * nt + i, U_CONV * UNIT // cw)),
                  pl.BlockSpec((hb, cw), lambda b, i: (jnp.maximum((b * nt + i) * (tm // hb) - 1, 0), 1)),
                  pl.BlockSpec((hb, cw), lambda b, i: (jnp.minimum((b * nt + i + 1) * (tm // hb), nhb - 1), 1)),
                  pl.BlockSpec((tm, 2 * UNIT), lambda b, i: (b * nt + i, U_AB * UNIT // (2 * UNIT))),
                  pl.BlockSpec((3, BRANCH_W), lambda b, i: (0, 0)),
                  pl.BlockSpec((3, 3 * BRANCH_W), lambda b, i: (0, 0))],
        out_specs=(ospec, ospec, ospec, ospec),
        compiler_params=_cparams(("parallel", "parallel")),
        name="convgate",
    )(proj, proj, proj, proj, conv_a, conv_h)


def _taps_kernel(cols_ref, w1c_ref, w1s_ref, w2_ref, w3_ref, dl_ref, taps_ref, asum_ref, *, seq, tr):
    i = pl.program_id(0)

    def source(m):
        return jnp.where(m < seq, m, 2 * seq - m).astype(F32)

    src_l = source(i * tr + lax.broadcasted_iota(jnp.int32, (1, tr), 1))
    t_l = src_l / (seq - 1.0)
    w_l = (2.0 * math.pi / seq) * src_l
    band = lax.broadcasted_iota(jnp.int32, (HYENA_BANDS, 1), 0).astype(F32)
    f = 1e-4 + band * ((HYENA_BANDS - 1.0 - 1e-4) / (HYENA_BANDS - 1.0))
    ang = f * w_l
    cols = cols_ref[...]
    pre = (cols[:, 0:1] * t_l
           + jnp.dot(w1c_ref[...], jnp.cos(ang), preferred_element_type=F32, precision=HIGHEST)
           + jnp.dot(w1s_ref[...], -jnp.sin(ang), preferred_element_type=F32, precision=HIGHEST)
           + cols[:, 1:2])
    h = jnp.sin(cols[:, 3:4] * pre)
    h = jnp.sin(cols[:, 4:5] * (jnp.dot(w2_ref[...], h, preferred_element_type=F32, precision=HIGHEST)
                                + cols[:, 2:3]))
    h = lax.dot_general(h, w3_ref[...], (((0,), (0,)), ((), ())), preferred_element_type=F32,
                        precision=HIGHEST)
    mrow = i * tr + lax.broadcasted_iota(jnp.int32, (tr, 1), 0)
    t = source(mrow) / (seq - 1.0)
    decay = jnp.exp(-t * jnp.abs(dl_ref[...]))
    out = h * (decay + HYENA_SHIFT)
    out = jnp.where(mrow == seq, 0.0, out)
    taps_ref[...] = out

    @pl.when(i == 0)
    def _():
        asum_ref[...] = jnp.zeros(asum_ref.shape, F32)

    asum_ref[...] += jnp.sum(jnp.abs(out), axis=0, keepdims=True)


def _hyena_taps(hp, *, seq, tr):
    w1, b1, w2, b2, w3, freq, deltas = hp
    hh = HYENA_HIDDEN
    cw = 2 * BRANCH_W
    w3d = w3.reshape(hh, 2, 2, BRANCH_W).transpose(2, 0, 1, 3).reshape(2, hh, cw)
    dld = deltas.transpose(1, 0, 2).reshape(2, 1, cw)
    cols = jnp.stack([w1[0], b1, b2, freq[0], freq[1]], axis=1)
    nt = 2 * seq // tr
    full = lambda shape: pl.BlockSpec(shape, lambda i: tuple(0 for _ in shape))
    return pl.pallas_call(
        functools.partial(_taps_kernel, seq=seq, tr=tr),
        out_shape=(jax.ShapeDtypeStruct((2 * seq, cw), F32), jax.ShapeDtypeStruct((1, cw), F32)),
        grid=(nt,),
        in_specs=[full((hh, 5)), full((hh, HYENA_BANDS)), full((hh, HYENA_BANDS)), full((hh, hh)),
                  pl.BlockSpec((None, hh, cw), lambda i: (i // (nt // 2), 0, 0)),
                  pl.BlockSpec((None, 1, cw), lambda i: (i // (nt // 2), 0, 0))],
        out_specs=(pl.BlockSpec((tr, cw), lambda i: (i, 0)), pl.BlockSpec((1, cw), lambda i: (0, 0))),
        compiler_params=_cparams(("arbitrary",)),
        name="hyena_taps",
    )(cols, w1[1:1 + HYENA_BANDS].T, w1[1 + HYENA_BANDS:].T, w2.T, w3d, dld)


def _lmm_kernel(a_ref, x_ref, *rest, gate):
    if gate:
        xg_ref, z_ref, bias_ref, o_ref = rest
    else:
        (o_ref,) = rest
    acc = jnp.dot(a_ref[...], x_ref[...].astype(BF16), preferred_element_type=F32)
    if gate:
        acc = xg_ref[...].astype(F32) * (acc + bias_ref[...] * z_ref[...].astype(F32))
    o_ref[...] = acc.astype(o_ref.dtype)


def _lmm(a, x, *, tn, out_dtype, gate_args=None, name):
    nb, k, n = x.shape
    ma = a.shape[0]
    in_specs = [pl.BlockSpec((ma, k), lambda b, j: (0, 0)), pl.BlockSpec((None, k, tn), lambda b, j: (b, 0, j))]
    args = [a, x]
    if gate_args is not None:
        xg, z, bias_t = gate_args
        in_specs += [pl.BlockSpec((None, ma, tn), lambda b, j: (b, 0, j)),
                     pl.BlockSpec((None, ma, tn), lambda b, j: (b, 0, j)),
                     pl.BlockSpec((1, tn), lambda b, j: (0, 0))]
        args += [xg, z, bias_t]
    return pl.pallas_call(
        functools.partial(_lmm_kernel, gate=gate_args is not None),
        out_shape=jax.ShapeDtypeStruct((nb, ma, n), out_dtype),
        grid=(nb, n // tn),
        in_specs=in_specs,
        out_specs=pl.BlockSpec((None, ma, tn), lambda b, j: (b, 0, j)),
        compiler_params=_cparams(("parallel", "parallel")),
        name=name,
    )(*args)


def _mid_kernel(a_ref, t_ref, *rest, conv):
    if conv:
        g_ref, o_ref, xbuf, ybuf = rest
    else:
        asum_ref, o_ref = rest
    kb, n2 = a_ref.shape[1], a_ref.shape[2]

    def forward(kk):
        a = jnp.concatenate([a_ref[0, kk], a_ref[1, kk]], axis=0)
        return jnp.dot(t_ref[kk], a, preferred_element_type=F32)

    if not conv:
        for kk in range(kb):
            xk = forward(kk) / (asum_ref[...] + 1e-6)
            o_ref[kk, 0] = xk[0:n2]
            o_ref[kk, 1] = xk[n2:]
        return

    for kk in range(kb):
        xbuf[kk] = forward(kk)
    for kk in range(kb):
        xr, xi = xbuf[kk, 0:n2], xbuf[kk, n2:]
        gr, gi = g_ref[kk, 0], g_ref[kk, 1]
        ybuf[kk, 0:n2] = (xr * gr - xi * gi).astype(BF16)
        ybuf[kk, n2:] = (xr * gi + xi * gr).astype(BF16)
    for kk in range(kb):
        bk = lax.dot_general(t_ref[kk], ybuf[kk], (((0,), (0,)), ((), ())), preferred_element_type=F32)
        o_ref[0, kk] = bk[0:n2].astype(o_ref.dtype)
        o_ref[1, kk] = bk[n2:].astype(o_ref.dtype)


def _mid_spectrum(a5, tmat, asum):
    _, _, n1, n2, cw = a5.shape
    kb = _pick(n1, (MID_K1_PER_STEP,))
    return pl.pallas_call(
        functools.partial(_mid_kernel, conv=False),
        out_shape=jax.ShapeDtypeStruct((n1, 2, n2, cw), F32),
        grid=(n1 // kb,),
        in_specs=[pl.BlockSpec((None, 2, kb, n2, cw), lambda k: (0, 0, k, 0, 0)),
                  pl.BlockSpec((kb, 2 * n2, 2 * n2), lambda k: (k, 0, 0)),
                  pl.BlockSpec((1, cw), lambda k: (0, 0))],
        out_specs=pl.BlockSpec((kb, 2, n2, cw), lambda k: (k, 0, 0, 0)),
        compiler_params=_cparams(("parallel",)),
        name="hyena_spectrum",
    )(a5, tmat, asum)


def _mid_conv(a5, tmat, gspec, order):
    nb, _, n1, n2, cw = a5.shape
    kb = _pick(n1, (MID_K1_PER_STEP,))
    return pl.pallas_call(
        functools.partial(_mid_kernel, conv=True),
        out_shape=jax.ShapeDtypeStruct((nb, 2, n1, n2, cw), BF16),
        grid=(n1 // kb, nb),
        in_specs=[pl.BlockSpec((None, 2, kb, n2, cw), lambda k, b: (b, 0, k, 0, 0)),
                  pl.BlockSpec((kb, 2 * n2, 2 * n2), lambda k, b: (k, 0, 0)),
                  pl.BlockSpec((kb, 2, n2, cw), lambda k, b: (k, 0, 0, order))],
        out_specs=pl.BlockSpec((None, 2, kb, n2, cw), lambda k, b: (b, 0, k, 0, 0)),
        scratch_shapes=[pltpu.VMEM((kb, 2 * n2, cw), F32), pltpu.VMEM((kb, 2 * n2, cw), BF16)],
        compiler_params=_cparams(("parallel", "arbitrary")),
        name="hyena_mid",
    )(a5, tmat, gspec)


def _dft_tables(seq):
    n = 2 * seq
    n2 = DFT_N2
    n1 = n // n2
    two_pi = 2.0 * math.pi

    def cs(num, den):
        ang = (num % den).astype(F32) * (two_pi / den)
        return jnp.cos(ang), jnp.sin(ang)

    k1 = jnp.arange(n1, dtype=jnp.int32)
    c, s = cs(k1[:, None] * jnp.arange(n1, dtype=jnp.int32)[None, :], n1)
    f1_full = jnp.concatenate([c, -s], axis=0)
    i2 = jnp.arange(n2, dtype=jnp.int32)
    num = i2[None, None, :] * k1[:, None, None] + n1 * (i2[None, :, None] * i2[None, None, :])
    c, s = cs(num, n)
    tre, tim = c, -s
    tmat = jnp.concatenate([jnp.concatenate([tre, -tim], axis=2),
                            jnp.concatenate([tim, tre], axis=2)], axis=1)
    c, s = cs(jnp.arange(n1 // 2, dtype=jnp.int32)[:, None] * k1[None, :], n1)
    fi = jnp.concatenate([c, -s], axis=1) * (1.0 / n)
    return f1_full.astype(BF16), f1_full[:, :n1 // 2].astype(BF16), tmat.astype(BF16), fi.astype(BF16)


def _dft_tables_small(seq):
    n = 2 * seq
    k = jnp.arange(n, dtype=jnp.int32)
    ang = ((k[:, None] * k[None, :]) % n).astype(F32) * (2.0 * math.pi / n)
    c, s = jnp.cos(ang), jnp.sin(ang)
    fwd = jnp.concatenate([c, -s], axis=0)
    inv = jnp.concatenate([c[:seq], -s[:seq]], axis=1) * (1.0 / n)
    return fwd.astype(BF16), fwd[:, :seq].astype(BF16), inv.astype(BF16)


def _ctxconv_kernel(z_ref, xg_ref, taps_ref, asum_ref, bias_ref, ff_ref, fh_ref, fi_ref, o_ref):
    n = ff_ref.shape[1]
    z = z_ref[...]
    g = jnp.dot(ff_ref[...], taps_ref[...].astype(BF16), preferred_element_type=F32) / (asum_ref[...] + 1e-6)
    xk = jnp.dot(fh_ref[...], z, preferred_element_type=F32)
    xr, xi = xk[0:n], xk[n:]
    gr, gi = g[0:n], g[n:]
    yk = jnp.concatenate([xr * gr - xi * gi, xr * gi + xi * gr], axis=0).astype(BF16)
    y = jnp.dot(fi_ref[...], yk, preferred_element_type=F32)
    o_ref[...] = (xg_ref[...].astype(F32) * (y + bias_ref[...] * z.astype(F32))).astype(o_ref.dtype)


def _ctx_longconv(z, xg, taps, asum, bias, tabs, order, *, bsz, seq):
    ff, fh, fi = tabs
    n = 2 * seq
    cw = BRANCH_W
    return pl.pallas_call(
        _ctxconv_kernel,
        out_shape=jax.ShapeDtypeStruct((bsz * seq, cw), BF16),
        grid=(bsz,),
        in_specs=[pl.BlockSpec((seq, cw), lambda b: (b, 0)),
                  pl.BlockSpec((seq, cw), lambda b: (b, 0)),
                  pl.BlockSpec((n, cw), lambda b: (0, order)),
                  pl.BlockSpec((1, cw), lambda b: (0, order)),
                  pl.BlockSpec((1, cw), lambda b: (0, 0)),
                  pl.BlockSpec((2 * n, n), lambda b: (0, 0)),
                  pl.BlockSpec((2 * n, seq), lambda b: (0, 0)),
                  pl.BlockSpec((seq, 2 * n), lambda b: (0, 0))],
        out_specs=pl.BlockSpec((seq, cw), lambda b: (b, 0)),
        compiler_params=_cparams(("parallel",)),
        name="hyena_ctx_conv",
    )(z, xg, taps, asum, bias, ff, fh, fi)


def _hyena_long(v0, x1, x2, taps, asum, hy_bias, tabs, *, bsz, seq):
    f1_full, f1_half, tmat, fi = tabs
    n2 = DFT_N2
    n1 = 2 * seq // n2
    cw = BRANCH_W
    tn = 4096
    at = _lmm(f1_full, taps.reshape(1, n1, n2 * 2 * cw), tn=tn, out_dtype=BF16, name="taps_dft1")
    gspec = _mid_spectrum(at.reshape(1, 2, n1, n2, 2 * cw), tmat, asum)

    def conv(z, xg, order):
        z2 = z.reshape(bsz, n1 // 2, n2 * cw)
        a = _lmm(f1_half, z2, tn=tn, out_dtype=BF16, name="hyena_dft1")
        bk = _mid_conv(a.reshape(bsz, 2, n1, n2, cw), tmat, gspec, order)
        bias_t = jnp.tile(hy_bias[order].reshape(1, cw), (1, tn // cw))
        y = _lmm(fi, bk.reshape(bsz, 2 * n1, n2 * cw), tn=tn, out_dtype=BF16,
                 gate_args=(xg.reshape(bsz, n1 // 2, n2 * cw), z2, bias_t), name="hyena_idft2")
        return y.reshape(bsz * seq, cw)

    return conv(conv(v0, x1, 0), x2, 1)


def _s5_kernel(u_ref, bre_ref, bim_ref, cre_ref, cim_ref, are_ref, aim_ref, s0_ref,
               y_ref, sfin_ref,
               uf, lhs, bur, bui, ybuf, pw_r, pw_i, car_r, car_i, st_r, st_i, *, tseg, rev):
    i = pl.program_id(1)
    sw = S5_SW
    gw = sw // S5_SUPER
    a_re = are_ref[...]
    a_im = aim_ref[...]

    @pl.when(jnp.logical_and(pl.program_id(0) == 0, i == 0))
    def _():
        def pbody(r, carry):
            cr, ci = carry
            pw_r[pl.ds(r, 1), :] = cr
            pw_i[pl.ds(r, 1), :] = ci
            return cr * a_re - ci * a_im, cr * a_im + ci * a_re
        lax.fori_loop(0, tseg, pbody, (a_re, a_im))

    @pl.when(i == 0)
    def _():
        st_r[...] = s0_ref[:, 0:sw]
        st_i[...] = s0_ref[:, sw:]

    for sg in range(S5_SUPER):
        uf[sg] = u_ref[:, sg * 128:(sg + 1) * 128].astype(F32)

    def gather(r, _):
        for sg in range(S5_SUPER):
            lhs[sg, pl.ds(pl.multiple_of(r * 8, 8), 8), :] = uf[sg, pl.ds(r, 8, stride=tseg), :]
        return 0
    lax.fori_loop(0, tseg, gather, 0, unroll=S5_UNROLL)

    for sg in range(S5_SUPER):
        lb = lhs[sg].astype(BF16)
        bur[:, sg * gw:(sg + 1) * gw] = jnp.dot(lb, bre_ref[sg], preferred_element_type=F32)
        bui[:, sg * gw:(sg + 1) * gw] = jnp.dot(lb, bim_ref[sg], preferred_element_type=F32)

    for sg in range(S5_SUPER):
        cols = slice(sg * gw, (sg + 1) * gw)
        ar = jnp.broadcast_to(a_re[:, cols], (8, gw))
        ai = jnp.broadcast_to(a_im[:, cols], (8, gw))

        def sbody(k, carry, cols=cols, ar=ar, ai=ai):
            sr, si = carry
            r = (tseg - 1 - k) if rev else k
            rows = pl.ds(pl.multiple_of(r * 8, 8), 8)
            nr = ar * sr - ai * si + bur[rows, cols]
            ni = ar * si + ai * sr + bui[rows, cols]
            bur[rows, cols] = nr
            bui[rows, cols] = ni
            return nr, ni
        zero = jnp.zeros((8, gw), F32)
        lax.fori_loop(0, tseg, sbody, (zero, zero))

    at_r = pw_r[tseg - 1:tseg, :]
    at_i = pw_i[tseg - 1:tseg, :]
    end_row = 0 if rev else (tseg - 1) * 8
    cr = st_r[...]
    ci = st_i[...]
    order = range(7, -1, -1) if rev else range(8)
    for s in order:
        car_r[s:s + 1, :] = cr
        car_i[s:s + 1, :] = ci
        er = bur[end_row + s:end_row + s + 1, :]
        ei = bui[end_row + s:end_row + s + 1, :]
        cr, ci = er + at_r * cr - at_i * ci, ei + at_r * ci + at_i * cr
    st_r[...] = cr
    st_i[...] = ci
    sfin_ref[:, 0:sw] = cr
    sfin_ref[:, sw:] = ci

    for sg in range(S5_SUPER):
        cols = slice(sg * gw, (sg + 1) * gw)
        kr = car_r[:, cols]
        kim = car_i[:, cols]

        def fbody(r, _, cols=cols, kr=kr, kim=kim):
            pidx = (tseg - 1 - r) if rev else r
            pr = pw_r[pl.ds(pidx, 1), cols]
            pi_ = pw_i[pl.ds(pidx, 1), cols]
            rows = pl.ds(pl.multiple_of(r * 8, 8), 8)
            bur[rows, cols] = bur[rows, cols] + (pr * kr - pi_ * kim)
            bui[rows, cols] = bui[rows, cols] + (pr * kim + pi_ * kr)
            return 0
        lax.fori_loop(0, tseg, fbody, 0, unroll=S5_UNROLL)

    for sg in range(S5_SUPER):
        cols = slice(sg * gw, (sg + 1) * gw)
        ybuf[sg] = (jnp.dot(bur[:, cols].astype(BF16), cre_ref[sg], preferred_element_type=F32)
                    + jnp.dot(bui[:, cols].astype(BF16), cim_ref[sg], preferred_element_type=F32))

    def scatter(r, _):
        for sg in range(S5_SUPER):
            y_ref[sg, pl.ds(r, 8, stride=tseg), :] = ybuf[sg, pl.ds(pl.multiple_of(r * 8, 8), 8), :]
        return 0
    lax.fori_loop(0, tseg, scatter, 0, unroll=S5_UNROLL)


def _s5_scan(proj, s5p, s0, *, bsz, seq, tseg, rev):
    bre, bim, cre, cim, are, aim = s5p
    tr = 8 * tseg
    nt = seq // tr
    sw = S5_SW
    gw = sw // S5_SUPER
    tile = (lambda b, i: (b * nt + (nt - 1 - i), U_U)) if rev else (lambda b, i: (b * nt + i, U_U))
    otile = (lambda b, i: (0, b * nt + (nt - 1 - i), 0)) if rev else (lambda b, i: (0, b * nt + i, 0))
    full = lambda shape: pl.BlockSpec(shape, lambda b, i: tuple(0 for _ in shape))
    return pl.pallas_call(
        functools.partial(_s5_kernel, tseg=tseg, rev=rev),
        out_shape=(jax.ShapeDtypeStruct((S5_SUPER, bsz * seq, 128), F32),
                   jax.ShapeDtypeStruct((bsz, 1, 2 * sw), F32)),
        grid=(bsz, nt),
        in_specs=[pl.BlockSpec((tr, UNIT), tile),
                  full((S5_SUPER, 128, gw)), full((S5_SUPER, 128, gw)),
                  full((S5_SUPER, gw, 128)), full((S5_SUPER, gw, 128)),
                  full((1, sw)), full((1, sw)),
                  pl.BlockSpec((None, 1, 2 * sw), lambda b, i: (b, 0, 0))],
        out_specs=(pl.BlockSpec((S5_SUPER, tr, 128), otile),
                   pl.BlockSpec((None, 1, 2 * sw), lambda b, i: (b, 0, 0))),
        scratch_shapes=[pltpu.VMEM((S5_SUPER, tr, 128), F32), pltpu.VMEM((S5_SUPER, tr, 128), F32),
                        pltpu.VMEM((tr, sw), F32), pltpu.VMEM((tr, sw), F32),
                        pltpu.VMEM((S5_SUPER, tr, 128), F32),
                        pltpu.VMEM((tseg, sw), F32), pltpu.VMEM((tseg, sw), F32),
                        pltpu.VMEM((8, sw), F32), pltpu.VMEM((8, sw), F32),
                        pltpu.VMEM((1, sw), F32), pltpu.VMEM((1, sw), F32)],
        compiler_params=_cparams(("arbitrary", "arbitrary")),
        name="s5_bwd" if rev else "s5_fwd",
    )(proj, bre, bim, cre, cim, are, aim, s0)


def _s5_params(a_re, a_im, log_step, b_re, b_im, c_re, c_im):
    g, p, ci = S5_GROUPS, S5_STATE, S5_GROUP
    dt = jnp.exp(log_step)[:, None]
    mag = jnp.exp(a_re * dt)
    ar, ai = mag * jnp.cos(a_im * dt), mag * jnp.sin(a_im * dt)
    den = a_re * a_re + a_im * a_im
    fr = ((ar - 1.0) * a_re + ai * a_im) / den
    fi = (ai * a_re - (ar - 1.0) * a_im) / den
    bbr = fr[..., None] * b_re - fi[..., None] * b_im
    bbi = fr[..., None] * b_im + fi[..., None] * b_re
    eye = jnp.eye(8, dtype=F32)

    def blockdiag_b(m):
        m4 = m.reshape(S5_SUPER, 8, p, ci)
        return jnp.einsum('sgpc,gh->sgchp', m4, eye).reshape(S5_SUPER, 8 * ci, 8 * p).astype(BF16)

    def blockdiag_c(m):
        m4 = m.reshape(S5_SUPER, 8, ci, p)
        return jnp.einsum('sgcp,gh->sgphc', m4, eye).reshape(S5_SUPER, 8 * p, 8 * ci).astype(BF16)

    return (blockdiag_b(bbr), blockdiag_b(bbi), blockdiag_c(c_re), blockdiag_c(-c_im),
            ar.reshape(1, g * p), ai.reshape(1, g * p))


def _merge_kernel(ya_ref, yh_ref, hg_ref, yatt_ref, attg_ref, ysf_ref, ysb_ref, u_ref, s5g_ref,
                  mg0_ref, mg1_ref, mg2_ref, mg3_ref, x_ref, mod_ref, s5d_ref, wglu_ref, wb_ref, wout_ref,
                  lng_ref, lnb_ref, o_ref, *, alpha):
    f = lambda r: r[...].astype(F32)
    y_a = f(ya_ref)
    y_h = f(yh_ref) * _silu(f(hg_ref))
    y_c = f(yatt_ref) * _silu(f(attg_ref))
    ys = jnp.concatenate([ysf_ref[sg] + ysb_ref[sg] for sg in range(S5_SUPER)], axis=1)
    y = ys + s5d_ref[...] * f(u_ref)
    zg = jax.nn.gelu(y)
    glu = jnp.dot(zg.astype(BF16), wglu_ref[...], preferred_element_type=F32)
    y_d = zg * _sigmoid(glu) * _silu(f(s5g_ref))
    mix = None
    for n, (yn, mg) in enumerate(((y_a, mg0_ref), (y_h, mg1_ref), (y_c, mg2_ref), (y_d, mg3_ref))):
        term = _sigmoid(f(mg)) * jnp.dot(yn.astype(BF16), wb_ref[n], preferred_element_type=F32)
        mix = term if mix is None else mix + term
    out = jnp.dot(mix.astype(BF16), wout_ref[...], preferred_element_type=F32)
    gate = mod_ref[2:3, :]
    r = alpha * x_ref[...] + gate * out
    mu = jnp.mean(r, axis=1, keepdims=True)
    rc = r - mu
    var = jnp.mean(rc * rc, axis=1, keepdims=True)
    o_ref[...] = rc * lax.rsqrt(var + LN_EPS) * lng_ref[...] + lnb_ref[...]


def _merge(ya, yh, yatt, ysf, ysb, proj, x2d, mod3, s5d, wglu, wb, wout, lng, lnb, *, seq, tm, alpha):
    m = x2d.shape[0]
    nt_seq = seq // tm
    nb = mod3.shape[0]
    mod_map = (lambda i: (i // nt_seq, 0, 0)) if nb > 1 else (lambda i: (0, 0, 0))
    row = lambda w_: pl.BlockSpec((tm, w_), lambda i: (i, 0))
    pcol = lambda unit: pl.BlockSpec((tm, UNIT), lambda i: (i, unit))
    mcol = lambda n: pl.BlockSpec((tm, 2 * UNIT), lambda i: (i, U_MERGE // 2 + n))
    full = lambda shape: pl.BlockSpec(shape, lambda i: tuple(0 for _ in shape))
    w_ = BRANCH_W
    s5row = pl.BlockSpec((S5_SUPER, tm, 128), lambda i: (0, i, 0))
    return pl.pallas_call(
        functools.partial(_merge_kernel, alpha=alpha),
        out_shape=jax.ShapeDtypeStruct((m, D_MODEL), F32),
        grid=(m // tm,),
        in_specs=[row(w_), row(w_), pcol(U_HG), row(w_), pcol(U_ATTG), s5row, s5row, pcol(U_U), pcol(U_S5G),
                  mcol(0), mcol(1), mcol(2), mcol(3), row(D_MODEL),
                  pl.BlockSpec((None, 3, D_MODEL), mod_map),
                  full((1, w_)), full((w_, w_)), full((N_BRANCH, w_, D_MODEL)), full((D_MODEL, D_MODEL)),
                  full((1, D_MODEL)), full((1, D_MODEL))],
        out_specs=row(D_MODEL),
        compiler_params=_cparams(("parallel",)),
        name="merge_out_norm",
    )(ya, yh, proj, yatt, proj, ysf, ysb, proj, proj, proj, proj, proj, proj, x2d, mod3,
      s5d, wglu, wb, wout, lng, lnb)


def _rope_tables(n_lat):
    rows = n_lat // GRID_W
    row = jnp.broadcast_to(jnp.arange(rows)[:, None], (rows, GRID_W)).reshape(-1)
    col = jnp.broadcast_to(jnp.arange(GRID_W)[None, :], (rows, GRID_W)).reshape(-1)
    half = HEAD_DIM // 2
    inv = 1.0 / (ROPE_BASE ** (jnp.arange(0, half, 2, dtype=F32) / half))
    ar, ac = row[:, None] * inv, col[:, None] * inv
    cos64 = jnp.concatenate([jnp.cos(ar), jnp.cos(ar), jnp.cos(ac), jnp.cos(ac)], axis=-1)
    sin64 = jnp.concatenate([-jnp.sin(ar), jnp.sin(ar), -jnp.sin(ac), jnp.sin(ac)], axis=-1)
    return jnp.tile(cos64, (1, 2)), jnp.tile(sin64, (1, 2))


def _pick(n, prefs):
    for t in prefs:
        if n % t == 0:
            return t
    return n


def kernel(x, c, ctx, c_ctx, w_mod, b_mod, w_in, conv_a, conv_h, hy_w1, hy_b1, hy_w2, hy_b2, hy_w3, hy_freq,
           hy_delta, hy_bias, lam_q1, lam_k1, lam_q2, lam_k2, attn_norm_g, s5_a_re, s5_a_im, s5_log_step,
           s5_b_re, s5_b_im, s5_c_re, s5_c_im, s5_d, s5_w_glu, w_branch, w_out, ln_g, ln_b):
    bsz, seq, d = x.shape
    seq_c = ctx.shape[1]
    depth = w_in.shape[0]
    alpha = (2.0 * depth) ** 0.25
    assert d == D_MODEL and seq % 1024 == 0 and seq_c % 256 == 0 and bsz + 1 <= 8

    cos_t, sin_t = _rope_tables(seq)
    dft_lat = _dft_tables(seq)
    dft_ctx = _dft_tables_small(seq_c)
    cvec = jnp.zeros((8, d), F32).at[0:bsz].set(c).at[bsz].set(c_ctx)
    tm_in = _pick(seq, (2048, 1024))
    tm_el = _pick(seq, (512,))
    tq = _pick(seq, (ATTN_TQ,))
    ck = _pick(seq, (1024, 512))
    tseg_lat = 128
    tseg_ctx = seq_c // 8

    x2 = x.reshape(bsz * seq, d)
    xc2 = ctx.reshape(bsz * seq_c, d)
    for l in range(depth):
        last = l == depth - 1
        lam_init = 0.8 - 0.6 * math.exp(-0.3 * l)
        mod = _modulation(cvec, w_mod, b_mod, l).reshape(8, 3, d)
        mod_lat, mod_ctx = mod[0:bsz], mod[bsz:bsz + 1]
        w_l = jnp.concatenate([w_in[l, :, u * UNIT:(u + 1) * UNIT] for u in UNIT_PERM], axis=1).astype(BF16)
        lamp = jnp.stack([lam_q1[l], lam_k1[l], lam_q2[l], lam_k2[l]], axis=0)
        g_att = attn_norm_g[l].reshape(1, V_DIM)
        s5f = _s5_params(s5_a_re[l, 0], s5_a_im[l, 0], s5_log_step[l, 0], s5_b_re[l, 0], s5_b_im[l, 0],
                         s5_c_re[l, 0], s5_c_im[l, 0])
        s5b = _s5_params(s5_a_re[l, 1], s5_a_im[l, 1], s5_log_step[l, 1], s5_b_re[l, 1], s5_b_im[l, 1],
                         s5_c_re[l, 1], s5_c_im[l, 1])
        hp = (hy_w1[l], hy_b1[l], hy_w2[l], hy_b2[l], hy_w3[l], hy_freq[l], hy_delta[l])
        wglu = s5_w_glu[l].astype(BF16)
        wb = w_branch[l].astype(BF16)
        wout = w_out[l].astype(BF16)
        s5d = s5_d[l].reshape(1, BRANCH_W)
        lng, lnb = ln_g[l].reshape(1, d), ln_b[l].reshape(1, d)

        ncols_c = 3 * 1024 if last else PROJ_W
        projc = _inproj(xc2, mod_ctx, w_l[:, :ncols_c], cos_t, sin_t, seq=seq_c, rope=False,
                        tm=bsz * seq_c, ncols=ncols_c)
        zero_state = jnp.zeros((bsz, 1, 2 * S5_SW), F32)
        ycf, scf = _s5_scan(projc, s5f, zero_state, bsz=bsz, seq=seq_c, tseg=tseg_ctx, rev=False)
        ycb, scb = _s5_scan(projc, s5b, zero_state, bsz=bsz, seq=seq_c, tseg=tseg_ctx, rev=True)

        proj = _inproj(x2, mod_lat, w_l, cos_t, sin_t, seq=seq, rope=True, tm=tm_in, ncols=PROJ_W)
        yatt = _attention_lat(proj, projc, lamp, g_att, bsz=bsz, seq=seq, seq_c=seq_c,
                              lam_init=lam_init, tq=tq, nsub=ATTN_SUBTILES, ck=ck)
        ysf, _ = _s5_scan(proj, s5f, scf, bsz=bsz, seq=seq, tseg=tseg_lat, rev=False)
        ysb, _ = _s5_scan(proj, s5b, scb, bsz=bsz, seq=seq, tseg=tseg_lat, rev=True)
        ya, v0, x1, x2h = _convgate(proj, conv_a[l], conv_h[l], bsz=bsz, seq=seq, tm=tm_el)
        taps, asum = _hyena_taps(hp, seq=seq, tr=512)
        yh = _hyena_long(v0, x1, x2h, taps, asum, hy_bias[l], dft_lat, bsz=bsz, seq=seq)
        x_new = _merge(ya, yh, yatt, ysf, ysb, proj, x2, mod_lat, s5d, wglu, wb, wout, lng, lnb,
                       seq=seq, tm=tm_el, alpha=alpha)

        if not last:
            yatt_c = _attention_ctx(projc, lamp, g_att, bsz=bsz, seq_c=seq_c, lam_init=lam_init)
            ya_c, v0c, x1c, x2c = _convgate(projc, conv_a[l], conv_h[l], bsz=bsz, seq=seq_c, tm=seq_c)
            taps_c, asum_c = _hyena_taps(hp, seq=seq_c, tr=seq_c)
            b0 = hy_bias[l, 0].reshape(1, BRANCH_W)
            b1 = hy_bias[l, 1].reshape(1, BRANCH_W)
            z1c = _ctx_longconv(v0c, x1c, taps_c, asum_c, b0, dft_ctx, 0, bsz=bsz, seq=seq_c)
            yh_c = _ctx_longconv(z1c, x2c, taps_c, asum_c, b1, dft_ctx, 1, bsz=bsz, seq=seq_c)
            xc2 = _merge(ya_c, yh_c, yatt_c, ycf, ycb, projc, xc2, mod_ctx, s5d, wglu, wb, wout, lng, lnb,
                         seq=seq_c, tm=seq_c, alpha=alpha)
        x2 = x_new
    return x2.reshape(bsz, seq, d)
```

```python
import functools
import math

import jax
import jax.numpy as jnp
from jax import lax
from jax.experimental import pallas as pl
from jax.experimental.pallas import tpu as pltpu

F32 = jnp.float32
BF16 = jnp.bfloat16
HIGHEST = lax.Precision.HIGHEST

D_MODEL = 1024
BRANCH_W = 512
N_HEADS = 4
HEAD_DIM = 64
V_DIM = 128
GRID_W = 64
ROPE_BASE = 10000.0
HYENA_BANDS = 16
HYENA_HIDDEN = 64
HYENA_SHIFT = 0.05
S5_GROUP = 16
S5_GROUPS = 32
S5_STATE = 64
S5_SUPER = 4
S5_SW = S5_GROUPS * S5_STATE
LN_EPS = 1e-5
N_BRANCH = 4
UNIT = 512
N_UNITS = 22
PROJ_W = N_UNITS * UNIT
UNIT_PERM = (0, 1, 11, 12, 2, 4, 5, 7, 8, 9, 3, 6, 13, 10, 14, 15, 16, 17, 18, 19, 20, 21)
U_K, U_V, U_Q, U_ATTG, U_U, U_CONV, U_AB, U_S5G, U_HG, U_MERGE = 0, 1, 2, 3, 4, 5, 10, 12, 13, 14
DFT_N2 = 128
MID_K1_PER_STEP = 8
S5_UNROLL = 8
ATTN_TQ = 256
ATTN_SUBTILES = 2
VMEM_LIMIT = 56 * 1024 * 1024


def _cparams(sem):
    return pltpu.CompilerParams(dimension_semantics=sem, vmem_limit_bytes=VMEM_LIMIT)


def _sigmoid(v):
    return 0.5 * jnp.tanh(0.5 * v) + 0.5


def _silu(v):
    return v * _sigmoid(v)


def _mod_kernel(s_ref, w_ref, b_ref, o_ref):
    s = _silu(s_ref[...])
    o_ref[...] = jnp.dot(s, w_ref[...], preferred_element_type=F32, precision=HIGHEST) + b_ref[...]


def _modulation(cvec, w_mod, b_mod, layer):
    depth, _, n = w_mod.shape
    tn = 512
    return pl.pallas_call(
        _mod_kernel,
        out_shape=jax.ShapeDtypeStruct((8, n), F32),
        grid=(n // tn,),
        in_specs=[pl.BlockSpec((8, D_MODEL), lambda j: (0, 0)),
                  pl.BlockSpec((None, D_MODEL, tn), lambda j: (layer, 0, j)),
                  pl.BlockSpec((None, 1, tn), lambda j: (layer, 0, j))],
        out_specs=pl.BlockSpec((8, tn), lambda j: (0, j)),
        compiler_params=_cparams(("parallel",)),
        name="modulation",
    )(cvec, w_mod, b_mod.reshape(depth, 1, n))


def _inproj_kernel(x_ref, mod_ref, w_ref, cos_ref, sin_ref, o_ref, h_ref, *, rope):
    j = pl.program_id(1)

    @pl.when(j == 0)
    def _():
        shift = mod_ref[0:1, :]
        scale = mod_ref[1:2, :]
        h_ref[...] = (x_ref[...] * (1.0 + scale) + shift).astype(BF16)

    def project():
        return jnp.dot(h_ref[...], w_ref[...], preferred_element_type=F32)

    if not rope:
        o_ref[...] = project().astype(o_ref.dtype)
        return

    @pl.when(j < 2)
    def _():
        acc = project()
        cs = cos_ref[...]
        sn = sin_ref[...]
        lane = lax.broadcasted_iota(jnp.int32, cs.shape, 1)
        first = (lane % 32) < 16
        for cb in range(UNIT // 128):
            t = acc[:, cb * 128:(cb + 1) * 128]
            partner = jnp.where(first, pltpu.roll(t, 128 - 16, 1), pltpu.roll(t, 16, 1))
            o_ref[:, cb * 128:(cb + 1) * 128] = (t * cs + partner * sn).astype(o_ref.dtype)
        o_ref[:, UNIT:] = acc[:, UNIT:].astype(o_ref.dtype)

    @pl.when(j >= 2)
    def _():
        o_ref[...] = project().astype(o_ref.dtype)


def _inproj(x2d, mod3, w, cos_t, sin_t, *, seq, rope, tm, ncols):
    m = x2d.shape[0]
    tn = 1024
    nt_seq = max(seq // tm, 1)
    nb = mod3.shape[0]
    mod_map = (lambda i, j: (i // nt_seq, 0, 0)) if nb > 1 else (lambda i, j: (0, 0, 0))
    return pl.pallas_call(
        functools.partial(_inproj_kernel, rope=rope),
        out_shape=jax.ShapeDtypeStruct((m, ncols), BF16),
        grid=(m // tm, ncols // tn),
        in_specs=[pl.BlockSpec((tm, D_MODEL), lambda i, j: (i, 0)),
                  pl.BlockSpec((None, 3, D_MODEL), mod_map),
                  pl.BlockSpec((D_MODEL, tn), lambda i, j: (0, j)),
                  pl.BlockSpec((tm, 128), lambda i, j: (i % nt_seq, 0)),
                  pl.BlockSpec((tm, 128), lambda i, j: (i % nt_seq, 0))],
        out_specs=pl.BlockSpec((tm, tn), lambda i, j: (i, j)),
        scratch_shapes=[pltpu.VMEM((tm, D_MODEL), BF16)],
        compiler_params=_cparams(("parallel", "arbitrary")),
        name="inproj_rope" if rope else "inproj_ctx",
    )(x2d, mod3, w, cos_t, sin_t)


def _head_output(o1, o2, lamp_ref, g_ref, lam_init):
    lp = lamp_ref[...]
    lam = (jnp.exp(jnp.sum(lp[0:1] * lp[1:2], axis=1, keepdims=True))
           - jnp.exp(jnp.sum(lp[2:3] * lp[3:4], axis=1, keepdims=True)) + lam_init)
    od = o1 - lam * o2
    return od * lax.rsqrt(jnp.mean(od * od, axis=1, keepdims=True) + 1e-5) * g_ref[...] * (1.0 - lam_init)


def _attn_ctx_kernel(q_ref, kc_ref, vc_ref, lamp_ref, g_ref, o_ref, *, tq, lam_init):
    q = q_ref[...].astype(F32) * (HEAD_DIM ** -0.5)
    lane = lax.broadcasted_iota(jnp.int32, q.shape, 1)
    qz = jnp.concatenate([jnp.where(lane < HEAD_DIM, q, 0.0), jnp.where(lane >= HEAD_DIM, q, 0.0)],
                         axis=0).astype(BF16)
    s = lax.dot_general(qz, kc_ref[...], (((1,), (1,)), ((), ())), preferred_element_type=F32)
    p = jnp.exp(s - jnp.max(s, axis=1, keepdims=True))
    l = jnp.sum(p, axis=1, keepdims=True)
    o = jnp.dot(p.astype(BF16), vc_ref[...], preferred_element_type=F32) / l
    o_ref[...] = _head_output(o[0:tq], o[tq:], lamp_ref, g_ref, lam_init).astype(o_ref.dtype)


def _attention_ctx(proj_c, lamp, g, *, bsz, seq_c, lam_init):
    hq = U_Q * UNIT // 128
    hk = U_K * UNIT // 128
    hv = U_V * UNIT // 128
    return pl.pallas_call(
        functools.partial(_attn_ctx_kernel, tq=seq_c, lam_init=lam_init),
        out_shape=jax.ShapeDtypeStruct((bsz * seq_c, N_HEADS * V_DIM), BF16),
        grid=(bsz, N_HEADS),
        in_specs=[pl.BlockSpec((seq_c, 128), lambda b, h: (b, hq + h)),
                  pl.BlockSpec((seq_c, 128), lambda b, h: (b, hk + h)),
                  pl.BlockSpec((seq_c, 128), lambda b, h: (b, hv + h)),
                  pl.BlockSpec((4, HEAD_DIM), lambda b, h: (0, 0)),
                  pl.BlockSpec((1, V_DIM), lambda b, h: (0, 0))],
        out_specs=pl.BlockSpec((seq_c, V_DIM), lambda b, h: (b, h)),
        compiler_params=_cparams(("parallel", "parallel")),
        name="diff_attn_ctx",
    )(proj_c, proj_c, proj_c, lamp, g)


def _attn_lat_kernel(q_ref, k_ref, v_ref, kc_ref, vc_ref, lamp_ref, g_ref, o_ref,
                     qz, vext, vcext, sbuf, pbuf, abuf, m_s, acc, *, tq, nsub, ck, n_chunks, lam_init):
    @pl.when(pl.program_id(2) == 0)
    def _():
        vext[:, 0:V_DIM] = v_ref[...]
        vext[:, V_DIM:] = jnp.ones((vext.shape[0], V_DIM), BF16)
        vcext[:, 0:V_DIM] = vc_ref[...]
        vcext[:, V_DIM:] = jnp.ones((vcext.shape[0], V_DIM), BF16)

    subs = range(nsub)
    for h in subs:
        q = q_ref[h * tq:(h + 1) * tq, :].astype(F32) * (HEAD_DIM ** -0.5 * math.log2(math.e))
        lane = lax.broadcasted_iota(jnp.int32, q.shape, 1)
        qz[h, 0:tq, :] = jnp.where(lane < HEAD_DIM, q, 0.0).astype(BF16)
        qz[h, tq:, :] = jnp.where(lane >= HEAD_DIM, q, 0.0).astype(BF16)

    def qk(h, kblk):
        return lax.dot_general(qz[h], kblk, (((1,), (1,)), ((), ())), preferred_element_type=F32)

    def kchunk(c):
        return k_ref[pl.ds(pl.multiple_of(c * ck, ck), ck), :]

    def vchunk(c):
        return vext[pl.ds(pl.multiple_of(c * ck, ck), ck), :]

    def score(slot, c):
        kblk = kchunk(c)
        for h in subs:
            sbuf[h, slot] = qk(h, kblk)

    def softmax(slot):
        for h in subs:
            s = sbuf[h, slot]
            m_prev = m_s[h]
            m_new = jnp.maximum(m_prev, jnp.max(s, axis=1, keepdims=True))
            abuf[h, slot] = jnp.exp2(m_prev - m_new)
            pbuf[h, slot] = jnp.exp2(s - m_new).astype(BF16)
            m_s[h] = m_new

    def pv(slot, c):
        vblk = vchunk(c)
        for h in subs:
            acc[h] = abuf[h, slot] * acc[h] + jnp.dot(pbuf[h, slot], vblk, preferred_element_type=F32)

    for h in subs:
        s = qk(h, kc_ref[...])
        m0 = jnp.max(s, axis=1, keepdims=True)
        m_s[h] = m0
        acc[h] = jnp.dot(jnp.exp2(s - m0).astype(BF16), vcext[...], preferred_element_type=F32)

    score(0, 0)
    softmax(0)
    score(1, 1)

    def body(j, _):
        a = 2 * j
        pv(0, a)
        score(0, a + 2)
        softmax(1)
        pv(1, a + 1)
        score(1, a + 3)
        softmax(0)
        return 0
    lax.fori_loop(0, n_chunks // 2 - 1, body, 0)

    pv(0, n_chunks - 2)
    softmax(1)
    pv(1, n_chunks - 1)

    for h in subs:
        a_ = acc[h]
        o = a_[:, 0:V_DIM] / a_[:, V_DIM:]
        o_ref[h * tq:(h + 1) * tq, :] = _head_output(o[0:tq], o[tq:], lamp_ref, g_ref,
                                                     lam_init).astype(o_ref.dtype)


def _attention_lat(proj, proj_c, lamp, g, *, bsz, seq, seq_c, lam_init, tq, nsub, ck):
    tqs = tq * nsub
    nq = seq // tqs
    n_chunks = seq // ck
    assert n_chunks % 2 == 0 and n_chunks >= 2
    hq = U_Q * UNIT // 128
    hk = U_K * UNIT // 128
    hv = U_V * UNIT // 128
    return pl.pallas_call(
        functools.partial(_attn_lat_kernel, tq=tq, nsub=nsub, ck=ck, n_chunks=n_chunks, lam_init=lam_init),
        out_shape=jax.ShapeDtypeStruct((bsz * seq, N_HEADS * V_DIM), BF16),
        grid=(bsz, N_HEADS, nq),
        in_specs=[pl.BlockSpec((tqs, 128), lambda b, h, qi: (b * nq + qi, hq + h)),
                  pl.BlockSpec((seq, 128), lambda b, h, qi: (b, hk + h)),
                  pl.BlockSpec((seq, 128), lambda b, h, qi: (b, hv + h)),
                  pl.BlockSpec((seq_c, 128), lambda b, h, qi: (b, hk + h)),
                  pl.BlockSpec((seq_c, 128), lambda b, h, qi: (b, hv + h)),
                  pl.BlockSpec((4, HEAD_DIM), lambda b, h, qi: (0, 0)),
                  pl.BlockSpec((1, V_DIM), lambda b, h, qi: (0, 0))],
        out_specs=pl.BlockSpec((tqs, V_DIM), lambda b, h, qi: (b * nq + qi, h)),
        scratch_shapes=[pltpu.VMEM((nsub, 2 * tq, 128), BF16),
                        pltpu.VMEM((seq, 2 * V_DIM), BF16), pltpu.VMEM((seq_c, 2 * V_DIM), BF16),
                        pltpu.VMEM((nsub, 2, 2 * tq, ck), F32), pltpu.VMEM((nsub, 2, 2 * tq, ck), BF16),
                        pltpu.VMEM((nsub, 2, 2 * tq, 1), F32), pltpu.VMEM((nsub, 2 * tq, 1), F32),
                        pltpu.VMEM((nsub, 2 * tq, 2 * V_DIM), F32)],
        compiler_params=_cparams(("parallel", "parallel", "arbitrary")),
        name="diff_attn",
    )(proj, proj, proj, proj_c, proj_c, lamp, g)


def _conv3(cur, prv, nxt, w):
    tm = cur.shape[0]
    row = lax.broadcasted_iota(jnp.int32, cur.shape, 0)
    dn = jnp.where(row == 0, prv, pltpu.roll(cur, 1, 0))
    up = jnp.where(row == tm - 1, nxt, pltpu.roll(cur, tm - 1, 0))
    return dn * w[0:1] + cur * w[1:2] + up * w[2:3]


def _convgate_kernel(main_ref, prev_ref, next_ref, ab_ref, ca_ref, ch_ref,
                     ya_ref, v0_ref, x1_ref, x2_ref, *, nt):
    i = pl.program_id(1)
    w_ = BRANCH_W
    m = main_ref[...].astype(F32)
    pv = jnp.where(i > 0, prev_ref[15:16, :].astype(F32), 0.0)
    nx = jnp.where(i < nt - 1, next_ref[0:1, :].astype(F32), 0.0)
    ab = ab_ref[...].astype(F32)
    p = m[:, 0:w_] * m[:, w_:2 * w_]
    p_prev = pv[:, 0:w_] * pv[:, w_:2 * w_]
    p_next = nx[:, 0:w_] * nx[:, w_:2 * w_]
    ya = ab[:, 0:w_] * _conv3(p, p_prev, p_next, ca_ref[...]) * _silu(ab[:, w_:])
    ya_ref[...] = ya.astype(ya_ref.dtype)
    ch = ch_ref[...]
    for n, ref in enumerate((v0_ref, x1_ref, x2_ref)):
        lo, hi = (2 + n) * w_, (3 + n) * w_
        ref[...] = _conv3(m[:, lo:hi], pv[:, lo:hi], nx[:, lo:hi], ch[:, n * w_:(n + 1) * w_]).astype(ref.dtype)


def _convgate(proj, conv_a, conv_h, *, bsz, seq, tm):
    m = bsz * seq
    nt = seq // tm
    cw = 5 * UNIT
    hb = 16
    nhb = m // hb
    out = jax.ShapeDtypeStruct((m, BRANCH_W), BF16)
    ospec = pl.BlockSpec((tm, BRANCH_W), lambda b, i: (b * nt + i, 0))
    return pl.pallas_call(
        functools.partial(_convgate_kernel, nt=nt),
        out_shape=(out, out, out, out),
        grid=(bsz, nt),
        in_specs=[pl.BlockSpec((tm, cw), lambda b, i: (b * nt + i, U_CONV * UNIT // cw)),
                  pl.BlockSpec((hb, cw), lambda b, i: (jnp.maximum((b * nt + i) * (tm // hb) - 1, 0), 1)),
                  pl.BlockSpec((hb, cw), lambda b, i: (jnp.minimum((b * nt + i + 1) * (tm // hb), nhb - 1), 1)),
                  pl.BlockSpec((tm, 2 * UNIT), lambda b, i: (b * nt + i, U_AB * UNIT // (2 * UNIT))),
                  pl.BlockSpec((3, BRANCH_W), lambda b, i: (0, 0)),
                  pl.BlockSpec((3, 3 * BRANCH_W), lambda b, i: (0, 0))],
        out_specs=(ospec, ospec, ospec, ospec),
        compiler_params=_cparams(("parallel", "parallel")),
        name="convgate",
    )(proj, proj, proj, proj, conv_a, conv_h)


def _taps_kernel(cols_ref, w1c_ref, w1s_ref, w2_ref, w3_ref, dl_ref, taps_ref, asum_ref, *, seq, tr):
    i = pl.program_id(0)

    def source(m):
        return jnp.where(m < seq, m, 2 * seq - m).astype(F32)

    src_l = source(i * tr + lax.broadcasted_iota(jnp.int32, (1, tr), 1))
    t_l = src_l / (seq - 1.0)
    w_l = (2.0 * math.pi / seq) * src_l
    band = lax.broadcasted_iota(jnp.int32, (HYENA_BANDS, 1), 0).astype(F32)
    f = 1e-4 + band * ((HYENA_BANDS - 1.0 - 1e-4) / (HYENA_BANDS - 1.0))
    ang = f * w_l
    cols = cols_ref[...]
    pre = (cols[:, 0:1] * t_l
           + jnp.dot(w1c_ref[...], jnp.cos(ang), preferred_element_type=F32, precision=HIGHEST)
           + jnp.dot(w1s_ref[...], -jnp.sin(ang), preferred_element_type=F32, precision=HIGHEST)
           + cols[:, 1:2])
    h = jnp.sin(cols[:, 3:4] * pre)
    h = jnp.sin(cols[:, 4:5] * (jnp.dot(w2_ref[...], h, preferred_element_type=F32, precision=HIGHEST)
                                + cols[:, 2:3]))
    w3 = w3_ref[...]
    h_hi, w_hi = h.astype(BF16), w3.astype(BF16)
    h_lo, w_lo = (h - h_hi.astype(F32)).astype(BF16), (w3 - w_hi.astype(F32)).astype(BF16)
    tn_dot = lambda a, b: lax.dot_general(a, b, (((0,), (0,)), ((), ())), preferred_element_type=F32)
    h = tn_dot(h_hi, w_hi) + (tn_dot(h_lo, w_hi) + tn_dot(h_hi, w_lo))
    mrow = i * tr + lax.broadcasted_iota(jnp.int32, (tr, 1), 0)
    t = source(mrow) / (seq - 1.0)
    decay = jnp.exp(-t * jnp.abs(dl_ref[...]))
    out = h * (decay + HYENA_SHIFT)
    out = jnp.where(mrow == seq, 0.0, out)
    taps_ref[...] = out

    @pl.when(i == 0)
    def _():
        asum_ref[...] = jnp.zeros(asum_ref.shape, F32)

    asum_ref[...] += jnp.sum(jnp.abs(out), axis=0, keepdims=True)


def _hyena_taps(hp, *, seq, tr):
    w1, b1, w2, b2, w3, freq, deltas = hp
    hh = HYENA_HIDDEN
    cw = 2 * BRANCH_W
    w3d = w3.reshape(hh, 2, 2, BRANCH_W).transpose(2, 0, 1, 3).reshape(2, hh, cw)
    dld = deltas.transpose(1, 0, 2).reshape(2, 1, cw)
    cols = jnp.stack([w1[0], b1, b2, freq[0], freq[1]], axis=1)
    nt = 2 * seq // tr
    full = lambda shape: pl.BlockSpec(shape, lambda i: tuple(0 for _ in shape))
    return pl.pallas_call(
        functools.partial(_taps_kernel, seq=seq, tr=tr),
        out_shape=(jax.ShapeDtypeStruct((2 * seq, cw), F32), jax.ShapeDtypeStruct((1, cw), F32)),
        grid=(nt,),
        in_specs=[full((hh, 5)), full((hh, HYENA_BANDS)), full((hh, HYENA_BANDS)), full((hh, hh)),
                  pl.BlockSpec((None, hh, cw), lambda i: (i // (nt // 2), 0, 0)),
                  pl.BlockSpec((None, 1, cw), lambda i: (i // (nt // 2), 0, 0))],
        out_specs=(pl.BlockSpec((tr, cw), lambda i: (i, 0)), pl.BlockSpec((1, cw), lambda i: (0, 0))),
        compiler_params=_cparams(("arbitrary",)),
        name="hyena_taps",
    )(cols, w1[1:1 + HYENA_BANDS].T, w1[1 + HYENA_BANDS:].T, w2.T, w3d, dld)


def _lmm_kernel(a_ref, x_ref, *rest, gate):
    if gate:
        xg_ref, z_ref, bias_ref, o_ref = rest
    else:
        (o_ref,) = rest
    acc = jnp.dot(a_ref[...], x_ref[...].astype(BF16), preferred_element_type=F32)
    if gate:
        acc = xg_ref[...].astype(F32) * (acc + bias_ref[...] * z_ref[...].astype(F32))
    o_ref[...] = acc.astype(o_ref.dtype)


def _lmm(a, x, *, tn, out_dtype, gate_args=None, name):
    nb, k, n = x.shape
    ma = a.shape[0]
    in_specs = [pl.BlockSpec((ma, k), lambda b, j: (0, 0)), pl.BlockSpec((None, k, tn), lambda b, j: (b, 0, j))]
    args = [a, x]
    if gate_args is not None:
        xg, z, bias_t = gate_args
        in_specs += [pl.BlockSpec((None, ma, tn), lambda b, j: (b, 0, j)),
                     pl.BlockSpec((None, ma, tn), lambda b, j: (b, 0, j)),
                     pl.BlockSpec((1, tn), lambda b, j: (0, 0))]
        args += [xg, z, bias_t]
    return pl.pallas_call(
        functools.partial(_lmm_kernel, gate=gate_args is not None),
        out_shape=jax.ShapeDtypeStruct((nb, ma, n), out_dtype),
        grid=(nb, n // tn),
        in_specs=in_specs,
        out_specs=pl.BlockSpec((None, ma, tn), lambda b, j: (b, 0, j)),
        compiler_params=_cparams(("parallel", "parallel")),
        name=name,
    )(*args)


def _mid_kernel(a_ref, t_ref, *rest, conv):
    if conv:
        g_ref, o_ref, xbuf, ybuf = rest
    else:
        asum_ref, o_ref = rest
    kb, n2 = a_ref.shape[1], a_ref.shape[2]

    def forward(kk):
        a = jnp.concatenate([a_ref[0, kk], a_ref[1, kk]], axis=0)
        return jnp.dot(t_ref[kk], a, preferred_element_type=F32)

    if not conv:
        for kk in range(kb):
            xk = forward(kk) / (asum_ref[...] + 1e-6)
            o_ref[kk, 0] = xk[0:n2].astype(o_ref.dtype)
            o_ref[kk, 1] = xk[n2:].astype(o_ref.dtype)
        return

    for kk in range(kb):
        xbuf[kk] = forward(kk)
    for kk in range(kb):
        xr, xi = xbuf[kk, 0:n2], xbuf[kk, n2:]
        gr, gi = g_ref[kk, 0].astype(F32), g_ref[kk, 1].astype(F32)
        ybuf[kk, 0:n2] = (xr * gr - xi * gi).astype(BF16)
        ybuf[kk, n2:] = (xr * gi + xi * gr).astype(BF16)
    for kk in range(kb):
        bk = lax.dot_general(t_ref[kk], ybuf[kk], (((0,), (0,)), ((), ())), preferred_element_type=F32)
        o_ref[0, kk] = bk[0:n2].astype(o_ref.dtype)
        o_ref[1, kk] = bk[n2:].astype(o_ref.dtype)


def _mid_spectrum(a5, tmat, asum):
    _, _, n1, n2, cw = a5.shape
    kb = _pick(n1, (MID_K1_PER_STEP,))
    return pl.pallas_call(
        functools.partial(_mid_kernel, conv=False),
        out_shape=jax.ShapeDtypeStruct((n1, 2, n2, cw), BF16),
        grid=(n1 // kb,),
        in_specs=[pl.BlockSpec((None, 2, kb, n2, cw), lambda k: (0, 0, k, 0, 0)),
                  pl.BlockSpec((kb, 2 * n2, 2 * n2), lambda k: (k, 0, 0)),
                  pl.BlockSpec((1, cw), lambda k: (0, 0))],
        out_specs=pl.BlockSpec((kb, 2, n2, cw), lambda k: (k, 0, 0, 0)),
        compiler_params=_cparams(("parallel",)),
        name="hyena_spectrum",
    )(a5, tmat, asum)


def _mid_conv(a5, tmat, gspec, order):
    nb, _, n1, n2, cw = a5.shape
    kb = _pick(n1, (MID_K1_PER_STEP,))
    return pl.pallas_call(
        functools.partial(_mid_kernel, conv=True),
        out_shape=jax.ShapeDtypeStruct((nb, 2, n1, n2, cw), BF16),
        grid=(n1 // kb, nb),
        in_specs=[pl.BlockSpec((None, 2, kb, n2, cw), lambda k, b: (b, 0, k, 0, 0)),
                  pl.BlockSpec((kb, 2 * n2, 2 * n2), lambda k, b: (k, 0, 0)),
                  pl.BlockSpec((kb, 2, n2, cw), lambda k, b: (k, 0, 0, order))],
        out_specs=pl.BlockSpec((None, 2, kb, n2, cw), lambda k, b: (b, 0, k, 0, 0)),
        scratch_shapes=[pltpu.VMEM((kb, 2 * n2, cw), F32), pltpu.VMEM((kb, 2 * n2, cw), BF16)],
        compiler_params=_cparams(("parallel", "arbitrary")),
        name="hyena_mid",
    )(a5, tmat, gspec)


def _dft_tables(seq):
    n = 2 * seq
    n2 = DFT_N2
    n1 = n // n2
    two_pi = 2.0 * math.pi

    def cs(num, den):
        ang = (num % den).astype(F32) * (two_pi / den)
        return jnp.cos(ang), jnp.sin(ang)

    k1 = jnp.arange(n1, dtype=jnp.int32)
    c, s = cs(k1[:, None] * jnp.arange(n1, dtype=jnp.int32)[None, :], n1)
    f1_full = jnp.concatenate([c, -s], axis=0)
    i2 = jnp.arange(n2, dtype=jnp.int32)
    num = i2[None, None, :] * k1[:, None, None] + n1 * (i2[None, :, None] * i2[None, None, :])
    c, s = cs(num, n)
    tre, tim = c, -s
    tmat = jnp.concatenate([jnp.concatenate([tre, -tim], axis=2),
                            jnp.concatenate([tim, tre], axis=2)], axis=1)
    c, s = cs(jnp.arange(n1 // 2, dtype=jnp.int32)[:, None] * k1[None, :], n1)
    fi = jnp.concatenate([c, -s], axis=1) * (1.0 / n)
    return f1_full.astype(BF16), f1_full[:, :n1 // 2].astype(BF16), tmat.astype(BF16), fi.astype(BF16)


def _dft_tables_small(seq):
    n = 2 * seq
    k = jnp.arange(n, dtype=jnp.int32)
    ang = ((k[:, None] * k[None, :]) % n).astype(F32) * (2.0 * math.pi / n)
    c, s = jnp.cos(ang), jnp.sin(ang)
    fwd = jnp.concatenate([c, -s], axis=0)
    inv = jnp.concatenate([c[:seq], -s[:seq]], axis=1) * (1.0 / n)
    return fwd.astype(BF16), fwd[:, :seq].astype(BF16), inv.astype(BF16)


def _ctxconv_kernel(z_ref, xg_ref, taps_ref, asum_ref, bias_ref, ff_ref, fh_ref, fi_ref, o_ref):
    n = ff_ref.shape[1]
    z = z_ref[...]
    g = jnp.dot(ff_ref[...], taps_ref[...].astype(BF16), preferred_element_type=F32) / (asum_ref[...] + 1e-6)
    xk = jnp.dot(fh_ref[...], z, preferred_element_type=F32)
    xr, xi = xk[0:n], xk[n:]
    gr, gi = g[0:n], g[n:]
    yk = jnp.concatenate([xr * gr - xi * gi, xr * gi + xi * gr], axis=0).astype(BF16)
    y = jnp.dot(fi_ref[...], yk, preferred_element_type=F32)
    o_ref[...] = (xg_ref[...].astype(F32) * (y + bias_ref[...] * z.astype(F32))).astype(o_ref.dtype)


def _ctx_longconv(z, xg, taps, asum, bias, tabs, order, *, bsz, seq):
    ff, fh, fi = tabs
    n = 2 * seq
    cw = BRANCH_W
    return pl.pallas_call(
        _ctxconv_kernel,
        out_shape=jax.ShapeDtypeStruct((bsz * seq, cw), BF16),
        grid=(bsz,),
        in_specs=[pl.BlockSpec((seq, cw), lambda b: (b, 0)),
                  pl.BlockSpec((seq, cw), lambda b: (b, 0)),
                  pl.BlockSpec((n, cw), lambda b: (0, order)),
                  pl.BlockSpec((1, cw), lambda b: (0, order)),
                  pl.BlockSpec((1, cw), lambda b: (0, 0)),
                  pl.BlockSpec((2 * n, n), lambda b: (0, 0)),
                  pl.BlockSpec((2 * n, seq), lambda b: (0, 0)),
                  pl.BlockSpec((seq, 2 * n), lambda b: (0, 0))],
        out_specs=pl.BlockSpec((seq, cw), lambda b: (b, 0)),
        compiler_params=_cparams(("parallel",)),
        name="hyena_ctx_conv",
    )(z, xg, taps, asum, bias, ff, fh, fi)


def _hyena_long(v0, x1, x2, taps, asum, hy_bias, tabs, *, bsz, seq):
    f1_full, f1_half, tmat, fi = tabs
    n2 = DFT_N2
    n1 = 2 * seq // n2
    cw = BRANCH_W
    tn = 4096
    at = _lmm(f1_full, taps.reshape(1, n1, n2 * 2 * cw), tn=tn, out_dtype=BF16, name="taps_dft1")
    gspec = _mid_spectrum(at.reshape(1, 2, n1, n2, 2 * cw), tmat, asum)

    def conv(z, xg, order):
        z2 = z.reshape(bsz, n1 // 2, n2 * cw)
        a = _lmm(f1_half, z2, tn=tn, out_dtype=BF16, name="hyena_dft1")
        bk = _mid_conv(a.reshape(bsz, 2, n1, n2, cw), tmat, gspec, order)
        bias_t = jnp.tile(hy_bias[order].reshape(1, cw), (1, tn // cw))
        y = _lmm(fi, bk.reshape(bsz, 2 * n1, n2 * cw), tn=tn, out_dtype=BF16,
                 gate_args=(xg.reshape(bsz, n1 // 2, n2 * cw), z2, bias_t), name="hyena_idft2")
        return y.reshape(bsz * seq, cw)

    return conv(conv(v0, x1, 0), x2, 1)


def _s5_kernel(u_ref, bre_ref, bim_ref, cre_ref, cim_ref, are_ref, aim_ref, s0_ref,
               y_ref, sfin_ref,
               uf, lhs, bur, bui, ybuf, pw_r, pw_i, car_r, car_i, st_r, st_i, *, tseg, rev):
    i = pl.program_id(1)
    sw = S5_SW
    gw = sw // S5_SUPER
    a_re = are_ref[...]
    a_im = aim_ref[...]

    @pl.when(jnp.logical_and(pl.program_id(0) == 0, i == 0))
    def _():
        def pbody(r, carry):
            cr, ci = carry
            pw_r[pl.ds(r, 1), :] = cr
            pw_i[pl.ds(r, 1), :] = ci
            return cr * a_re - ci * a_im, cr * a_im + ci * a_re
        lax.fori_loop(0, tseg, pbody, (a_re, a_im))

    @pl.when(i == 0)
    def _():
        st_r[...] = s0_ref[:, 0:sw]
        st_i[...] = s0_ref[:, sw:]

    for sg in range(S5_SUPER):
        uf[sg] = u_ref[:, sg * 128:(sg + 1) * 128].astype(F32)

    def gather(r, _):
        for sg in range(S5_SUPER):
            lhs[sg, pl.ds(pl.multiple_of(r * 8, 8), 8), :] = uf[sg, pl.ds(r, 8, stride=tseg), :]
        return 0
    lax.fori_loop(0, tseg, gather, 0, unroll=S5_UNROLL)

    for sg in range(S5_SUPER):
        lb = lhs[sg].astype(BF16)
        bur[:, sg * gw:(sg + 1) * gw] = jnp.dot(lb, bre_ref[sg], preferred_element_type=F32)
        bui[:, sg * gw:(sg + 1) * gw] = jnp.dot(lb, bim_ref[sg], preferred_element_type=F32)

    for sg in range(S5_SUPER):
        cols = slice(sg * gw, (sg + 1) * gw)
        ar = jnp.broadcast_to(a_re[:, cols], (8, gw))
        ai = jnp.broadcast_to(a_im[:, cols], (8, gw))

        def sbody(k, carry, cols=cols, ar=ar, ai=ai):
            sr, si = carry
            r = (tseg - 1 - k) if rev else k
            rows = pl.ds(pl.multiple_of(r * 8, 8), 8)
            nr = ar * sr - ai * si + bur[rows, cols]
            ni = ar * si + ai * sr + bui[rows, cols]
            bur[rows, cols] = nr
            bui[rows, cols] = ni
            return nr, ni
        zero = jnp.zeros((8, gw), F32)
        lax.fori_loop(0, tseg, sbody, (zero, zero))

    at_r = pw_r[tseg - 1:tseg, :]
    at_i = pw_i[tseg - 1:tseg, :]
    end_row = 0 if rev else (tseg - 1) * 8
    cr = st_r[...]
    ci = st_i[...]
    order = range(7, -1, -1) if rev else range(8)
    for s in order:
        car_r[s:s + 1, :] = cr
        car_i[s:s + 1, :] = ci
        er = bur[end_row + s:end_row + s + 1, :]
        ei = bui[end_row + s:end_row + s + 1, :]
        cr, ci = er + at_r * cr - at_i * ci, ei + at_r * ci + at_i * cr
    st_r[...] = cr
    st_i[...] = ci
    sfin_ref[:, 0:sw] = cr
    sfin_ref[:, sw:] = ci

    for sg in range(S5_SUPER):
        cols = slice(sg * gw, (sg + 1) * gw)
        kr = car_r[:, cols]
        kim = car_i[:, cols]

        def fbody(r, _, cols=cols, kr=kr, kim=kim):
            pidx = (tseg - 1 - r) if rev else r
            pr = pw_r[pl.ds(pidx, 1), cols]
            pi_ = pw_i[pl.ds(pidx, 1), cols]
            rows = pl.ds(pl.multiple_of(r * 8, 8), 8)
            bur[rows, cols] = bur[rows, cols] + (pr * kr - pi_ * kim)
            bui[rows, cols] = bui[rows, cols] + (pr * kim + pi_ * kr)
            return 0
        lax.fori_loop(0, tseg, fbody, 0, unroll=S5_UNROLL)

    for sg in range(S5_SUPER):
        cols = slice(sg * gw, (sg + 1) * gw)
        ybuf[sg] = (jnp.dot(bur[:, cols].astype(BF16), cre_ref[sg], preferred_element_type=F32)
                    + jnp.dot(bui[:, cols].astype(BF16), cim_ref[sg], preferred_element_type=F32))

    def scatter(r, _):
        for sg in range(S5_SUPER):
            y_ref[sg, pl.ds(r, 8, stride=tseg), :] = ybuf[sg, pl.ds(pl.multiple_of(r * 8, 8), 8), :]
        return 0
    lax.fori_loop(0, tseg, scatter, 0, unroll=S5_UNROLL)


def _s5_scan(proj, s5p, s0, *, bsz, seq, tseg, rev):
    bre, bim, cre, cim, are, aim = s5p
    tr = 8 * tseg
    nt = seq // tr
    sw = S5_SW
    gw = sw // S5_SUPER
    tile = (lambda b, i: (b * nt + (nt - 1 - i), U_U)) if rev else (lambda b, i: (b * nt + i, U_U))
    otile = (lambda b, i: (0, b * nt + (nt - 1 - i), 0)) if rev else (lambda b, i: (0, b * nt + i, 0))
    full = lambda shape: pl.BlockSpec(shape, lambda b, i: tuple(0 for _ in shape))
    return pl.pallas_call(
        functools.partial(_s5_kernel, tseg=tseg, rev=rev),
        out_shape=(jax.ShapeDtypeStruct((S5_SUPER, bsz * seq, 128), F32),
                   jax.ShapeDtypeStruct((bsz, 1, 2 * sw), F32)),
        grid=(bsz, nt),
        in_specs=[pl.BlockSpec((tr, UNIT), tile),
                  full((S5_SUPER, 128, gw)), full((S5_SUPER, 128, gw)),
                  full((S5_SUPER, gw, 128)), full((S5_SUPER, gw, 128)),
                  full((1, sw)), full((1, sw)),
                  pl.BlockSpec((None, 1, 2 * sw), lambda b, i: (b, 0, 0))],
        out_specs=(pl.BlockSpec((S5_SUPER, tr, 128), otile),
                   pl.BlockSpec((None, 1, 2 * sw), lambda b, i: (b, 0, 0))),
        scratch_shapes=[pltpu.VMEM((S5_SUPER, tr, 128), F32), pltpu.VMEM((S5_SUPER, tr, 128), F32),
                        pltpu.VMEM((tr, sw), F32), pltpu.VMEM((tr, sw), F32),
                        pltpu.VMEM((S5_SUPER, tr, 128), F32),
                        pltpu.VMEM((tseg, sw), F32), pltpu.VMEM((tseg, sw), F32),
                        pltpu.VMEM((8, sw), F32), pltpu.VMEM((8, sw), F32),
                        pltpu.VMEM((1, sw), F32), pltpu.VMEM((1, sw), F32)],
        compiler_params=_cparams(("arbitrary", "arbitrary")),
        name="s5_bwd" if rev else "s5_fwd",
    )(proj, bre, bim, cre, cim, are, aim, s0)


def _s5_params(a_re, a_im, log_step, b_re, b_im, c_re, c_im):
    g, p, ci = S5_GROUPS, S5_STATE, S5_GROUP
    dt = jnp.exp(log_step)[:, None]
    mag = jnp.exp(a_re * dt)
    ar, ai = mag * jnp.cos(a_im * dt), mag * jnp.sin(a_im * dt)
    den = a_re * a_re + a_im * a_im
    fr = ((ar - 1.0) * a_re + ai * a_im) / den
    fi = (ai * a_re - (ar - 1.0) * a_im) / den
    bbr = fr[..., None] * b_re - fi[..., None] * b_im
    bbi = fr[..., None] * b_im + fi[..., None] * b_re
    eye = jnp.eye(8, dtype=F32)

    def blockdiag_b(m):
        m4 = m.reshape(S5_SUPER, 8, p, ci)
        return jnp.einsum('sgpc,gh->sgchp', m4, eye).reshape(S5_SUPER, 8 * ci, 8 * p).astype(BF16)

    def blockdiag_c(m):
        m4 = m.reshape(S5_SUPER, 8, ci, p)
        return jnp.einsum('sgcp,gh->sgphc', m4, eye).reshape(S5_SUPER, 8 * p, 8 * ci).astype(BF16)

    return (blockdiag_b(bbr), blockdiag_b(bbi), blockdiag_c(c_re), blockdiag_c(-c_im),
            ar.reshape(1, g * p), ai.reshape(1, g * p))


def _merge_kernel(ya_ref, yh_ref, hg_ref, yatt_ref, attg_ref, ysf_ref, ysb_ref, u_ref, s5g_ref,
                  mg0_ref, mg1_ref, mg2_ref, mg3_ref, x_ref, mod_ref, s5d_ref, wglu_ref, wb_ref, wout_ref,
                  lng_ref, lnb_ref, o_ref, *, alpha):
    f = lambda r: r[...].astype(F32)
    y_a = f(ya_ref)
    y_h = f(yh_ref) * _silu(f(hg_ref))
    y_c = f(yatt_ref) * _silu(f(attg_ref))
    ys = jnp.concatenate([ysf_ref[sg] + ysb_ref[sg] for sg in range(S5_SUPER)], axis=1)
    y = ys + s5d_ref[...] * f(u_ref)
    zg = jax.nn.gelu(y)
    glu = jnp.dot(zg.astype(BF16), wglu_ref[...], preferred_element_type=F32)
    y_d = zg * _sigmoid(glu) * _silu(f(s5g_ref))
    mix = None
    for n, (yn, mg) in enumerate(((y_a, mg0_ref), (y_h, mg1_ref), (y_c, mg2_ref), (y_d, mg3_ref))):
        term = _sigmoid(f(mg)) * jnp.dot(yn.astype(BF16), wb_ref[n], preferred_element_type=F32)
        mix = term if mix is None else mix + term
    out = jnp.dot(mix.astype(BF16), wout_ref[...], preferred_element_type=F32)
    gate = mod_ref[2:3, :]
    r = alpha * x_ref[...] + gate * out
    mu = jnp.mean(r, axis=1, keepdims=True)
    rc = r - mu
    var = jnp.mean(rc * rc, axis=1, keepdims=True)
    o_ref[...] = rc * lax.rsqrt(var + LN_EPS) * lng_ref[...] + lnb_ref[...]


def _merge(ya, yh, yatt, ysf, ysb, proj, x2d, mod3, s5d, wglu, wb, wout, lng, lnb, *, seq, tm, alpha):
    m = x2d.shape[0]
    nt_seq = seq // tm
    nb = mod3.shape[0]
    mod_map = (lambda i: (i // nt_seq, 0, 0)) if nb > 1 else (lambda i: (0, 0, 0))
    row = lambda w_: pl.BlockSpec((tm, w_), lambda i: (i, 0))
    pcol = lambda unit: pl.BlockSpec((tm, UNIT), lambda i: (i, unit))
    mcol = lambda n: pl.BlockSpec((tm, 2 * UNIT), lambda i: (i, U_MERGE // 2 + n))
    full = lambda shape: pl.BlockSpec(shape, lambda i: tuple(0 for _ in shape))
    w_ = BRANCH_W
    s5row = pl.BlockSpec((S5_SUPER, tm, 128), lambda i: (0, i, 0))
    return pl.pallas_call(
        functools.partial(_merge_kernel, alpha=alpha),
        out_shape=jax.ShapeDtypeStruct((m, D_MODEL), F32),
        grid=(m // tm,),
        in_specs=[row(w_), row(w_), pcol(U_HG), row(w_), pcol(U_ATTG), s5row, s5row, pcol(U_U), pcol(U_S5G),
                  mcol(0), mcol(1), mcol(2), mcol(3), row(D_MODEL),
                  pl.BlockSpec((None, 3, D_MODEL), mod_map),
                  full((1, w_)), full((w_, w_)), full((N_BRANCH, w_, D_MODEL)), full((D_MODEL, D_MODEL)),
                  full((1, D_MODEL)), full((1, D_MODEL))],
        out_specs=row(D_MODEL),
        compiler_params=_cparams(("parallel",)),
        name="merge_out_norm",
    )(ya, yh, proj, yatt, proj, ysf, ysb, proj, proj, proj, proj, proj, proj, x2d, mod3,
      s5d, wglu, wb, wout, lng, lnb)


def _rope_tables(n_lat):
    rows = n_lat // GRID_W
    row = jnp.broadcast_to(jnp.arange(rows)[:, None], (rows, GRID_W)).reshape(-1)
    col = jnp.broadcast_to(jnp.arange(GRID_W)[None, :], (rows, GRID_W)).reshape(-1)
    half = HEAD_DIM // 2
    inv = 1.0 / (ROPE_BASE ** (jnp.arange(0, half, 2, dtype=F32) / half))
    ar, ac = row[:, None] * inv, col[:, None] * inv
    cos64 = jnp.concatenate([jnp.cos(ar), jnp.cos(ar), jnp.cos(ac), jnp.cos(ac)], axis=-1)
    sin64 = jnp.concatenate([-jnp.sin(ar), jnp.sin(ar), -jnp.sin(ac), jnp.sin(ac)], axis=-1)
    return jnp.tile(cos64, (1, 2)), jnp.tile(sin64, (1, 2))


def _pick(n, prefs):
    for t in prefs:
        if n % t == 0:
            return t
    return n


def kernel(x, c, ctx, c_ctx, w_mod, b_mod, w_in, conv_a, conv_h, hy_w1, hy_b1, hy_w2, hy_b2, hy_w3, hy_freq,
           hy_delta, hy_bias, lam_q1, lam_k1, lam_q2, lam_k2, attn_norm_g, s5_a_re, s5_a_im, s5_log_step,
           s5_b_re, s5_b_im, s5_c_re, s5_c_im, s5_d, s5_w_glu, w_branch, w_out, ln_g, ln_b):
    bsz, seq, d = x.shape
    seq_c = ctx.shape[1]
    depth = w_in.shape[0]
    alpha = (2.0 * depth) ** 0.25
    assert d == D_MODEL and seq % 1024 == 0 and seq_c % 256 == 0 and bsz + 1 <= 8

    cos_t, sin_t = _rope_tables(seq)
    dft_lat = _dft_tables(seq)
    dft_ctx = _dft_tables_small(seq_c)
    cvec = jnp.zeros((8, d), F32).at[0:bsz].set(c).at[bsz].set(c_ctx)
    tm_in = _pick(seq, (2048, 1024))
    tm_el = _pick(seq, (512,))
    tq = _pick(seq, (ATTN_TQ,))
    ck = _pick(seq, (1024, 512))
    tseg_lat = 128
    tseg_ctx = seq_c // 8

    x2 = x.reshape(bsz * seq, d)
    xc2 = ctx.reshape(bsz * seq_c, d)
    for l in range(depth):
        last = l == depth - 1
        lam_init = 0.8 - 0.6 * math.exp(-0.3 * l)
        mod = _modulation(cvec, w_mod, b_mod, l).reshape(8, 3, d)
        mod_lat, mod_ctx = mod[0:bsz], mod[bsz:bsz + 1]
        w_l = jnp.concatenate([w_in[l, :, u * UNIT:(u + 1) * UNIT] for u in UNIT_PERM], axis=1).astype(BF16)
        lamp = jnp.stack([lam_q1[l], lam_k1[l], lam_q2[l], lam_k2[l]], axis=0)
        g_att = attn_norm_g[l].reshape(1, V_DIM)
        s5f = _s5_params(s5_a_re[l, 0], s5_a_im[l, 0], s5_log_step[l, 0], s5_b_re[l, 0], s5_b_im[l, 0],
                         s5_c_re[l, 0], s5_c_im[l, 0])
        s5b = _s5_params(s5_a_re[l, 1], s5_a_im[l, 1], s5_log_step[l, 1], s5_b_re[l, 1], s5_b_im[l, 1],
                         s5_c_re[l, 1], s5_c_im[l, 1])
        hp = (hy_w1[l], hy_b1[l], hy_w2[l], hy_b2[l], hy_w3[l], hy_freq[l], hy_delta[l])
        wglu = s5_w_glu[l].astype(BF16)
        wb = w_branch[l].astype(BF16)
        wout = w_out[l].astype(BF16)
        s5d = s5_d[l].reshape(1, BRANCH_W)
        lng, lnb = ln_g[l].reshape(1, d), ln_b[l].reshape(1, d)

        ncols_c = 3 * 1024 if last else PROJ_W
        projc = _inproj(xc2, mod_ctx, w_l[:, :ncols_c], cos_t, sin_t, seq=seq_c, rope=False,
                        tm=bsz * seq_c, ncols=ncols_c)
        zero_state = jnp.zeros((bsz, 1, 2 * S5_SW), F32)
        ycf, scf = _s5_scan(projc, s5f, zero_state, bsz=bsz, seq=seq_c, tseg=tseg_ctx, rev=False)
        ycb, scb = _s5_scan(projc, s5b, zero_state, bsz=bsz, seq=seq_c, tseg=tseg_ctx, rev=True)

        proj = _inproj(x2, mod_lat, w_l, cos_t, sin_t, seq=seq, rope=True, tm=tm_in, ncols=PROJ_W)
        yatt = _attention_lat(proj, projc, lamp, g_att, bsz=bsz, seq=seq, seq_c=seq_c,
                              lam_init=lam_init, tq=tq, nsub=ATTN_SUBTILES, ck=ck)
        ysf, _ = _s5_scan(proj, s5f, scf, bsz=bsz, seq=seq, tseg=tseg_lat, rev=False)
        ysb, _ = _s5_scan(proj, s5b, scb, bsz=bsz, seq=seq, tseg=tseg_lat, rev=True)
        ya, v0, x1, x2h = _convgate(proj, conv_a[l], conv_h[l], bsz=bsz, seq=seq, tm=tm_el)
        taps, asum = _hyena_taps(hp, seq=seq, tr=512)
        yh = _hyena_long(v0, x1, x2h, taps, asum, hy_bias[l], dft_lat, bsz=bsz, seq=seq)
        x_new = _merge(ya, yh, yatt, ysf, ysb, proj, x2, mod_lat, s5d, wglu, wb, wout, lng, lnb,
                       seq=seq, tm=tm_el, alpha=alpha)

        if not last:
            yatt_c = _attention_ctx(projc, lamp, g_att, bsz=bsz, seq_c=seq_c, lam_init=lam_init)
            ya_c, v0c, x1c, x2c = _convgate(projc, conv_a[l], conv_h[l], bsz=bsz, seq=seq_c, tm=seq_c)
            taps_c, asum_c = _hyena_taps(hp, seq=seq_c, tr=seq_c)
            b0 = hy_bias[l, 0].reshape(1, BRANCH_W)
            b1 = hy_bias[l, 1].reshape(1, BRANCH_W)
            z1c = _ctx_longconv(v0c, x1c, taps_c, asum_c, b0, dft_ctx, 0, bsz=bsz, seq=seq_c)
            yh_c = _ctx_longconv(z1c, x2c, taps_c, asum_c, b1, dft_ctx, 1, bsz=bsz, seq=seq_c)
            xc2 = _merge(ya_c, yh_c, yatt_c, ycf, ycb, projc, xc2, mod_ctx, s5d, wglu, wb, wout, lng, lnb,
                         seq=seq_c, tm=seq_c, alpha=alpha)
        x2 = x_new
    return x2.reshape(bsz, seq, d)
```

```python
import functools
import math

import jax
import jax.numpy as jnp
from jax import lax
from jax.experimental import pallas as pl
from jax.experimental.pallas import tpu as pltpu

F32 = jnp.float32
BF16 = jnp.bfloat16
HIGHEST = lax.Precision.HIGHEST

D_MODEL = 1024
BRANCH_W = 512
N_HEADS = 4
HEAD_DIM = 64
V_DIM = 128
GRID_W = 64
ROPE_BASE = 10000.0
HYENA_BANDS = 16
HYENA_HIDDEN = 64
HYENA_SHIFT = 0.05
S5_GROUP = 16
S5_GROUPS = 32
S5_STATE = 64
S5_SUPER = 4
S5_SW = S5_GROUPS * S5_STATE
LN_EPS = 1e-5
N_BRANCH = 4
UNIT = 512
N_UNITS = 22
PROJ_W = N_UNITS * UNIT
UNIT_PERM = (0, 1, 11, 12, 2, 4, 5, 7, 8, 9, 3, 6, 13, 10, 14, 15, 16, 17, 18, 19, 20, 21)
U_K, U_V, U_Q, U_ATTG, U_U, U_CONV, U_AB, U_S5G, U_HG, U_MERGE = 0, 1, 2, 3, 4, 5, 10, 12, 13, 14
DFT_N2 = 128
MID_K1_PER_STEP = 8
S5_UNROLL = 8
ATTN_TQ = 256
ATTN_SUBTILES = 2
VMEM_LIMIT = 56 * 1024 * 1024


def _cparams(sem):
    return pltpu.CompilerParams(dimension_semantics=sem, vmem_limit_bytes=VMEM_LIMIT)


def _sigmoid(v):
    return 0.5 * jnp.tanh(0.5 * v) + 0.5


def _silu(v):
    return v * _sigmoid(v)


def _mod_kernel(s_ref, w_ref, b_ref, o_ref):
    s = _silu(s_ref[...])
    o_ref[...] = jnp.dot(s, w_ref[...], preferred_element_type=F32, precision=HIGHEST) + b_ref[...]


def _modulation(cvec, w_mod, b_mod, layer):
    depth, _, n = w_mod.shape
    tn = 512
    return pl.pallas_call(
        _mod_kernel,
        out_shape=jax.ShapeDtypeStruct((8, n), F32),
        grid=(n // tn,),
        in_specs=[pl.BlockSpec((8, D_MODEL), lambda j: (0, 0)),
                  pl.BlockSpec((None, D_MODEL, tn), lambda j: (layer, 0, j)),
                  pl.BlockSpec((None, 1, tn), lambda j: (layer, 0, j))],
        out_specs=pl.BlockSpec((8, tn), lambda j: (0, j)),
        compiler_params=_cparams(("parallel",)),
        name="modulation",
    )(cvec, w_mod, b_mod.reshape(depth, 1, n))


def _inproj_kernel(x_ref, mod_ref, w_ref, cos_ref, sin_ref, o_ref, h_ref, *, rope):
    j = pl.program_id(1)

    @pl.when(j == 0)
    def _():
        shift = mod_ref[0:1, :]
        scale = mod_ref[1:2, :]
        h_ref[...] = (x_ref[...] * (1.0 + scale) + shift).astype(BF16)

    def project():
        return jnp.dot(h_ref[...], w_ref[...], preferred_element_type=F32)

    if not rope:
        o_ref[...] = project().astype(o_ref.dtype)
        return

    @pl.when(j < 2)
    def _():
        acc = project()
        cs = cos_ref[...]
        sn = sin_ref[...]
        lane = lax.broadcasted_iota(jnp.int32, cs.shape, 1)
        first = (lane % 32) < 16
        for cb in range(UNIT // 128):
            t = acc[:, cb * 128:(cb + 1) * 128]
            partner = jnp.where(first, pltpu.roll(t, 128 - 16, 1), pltpu.roll(t, 16, 1))
            o_ref[:, cb * 128:(cb + 1) * 128] = (t * cs + partner * sn).astype(o_ref.dtype)
        o_ref[:, UNIT:] = acc[:, UNIT:].astype(o_ref.dtype)

    @pl.when(j >= 2)
    def _():
        o_ref[...] = project().astype(o_ref.dtype)


def _inproj(x2d, mod3, w, cos_t, sin_t, *, seq, rope, tm, ncols):
    m = x2d.shape[0]
    tn = 1024
    nt_seq = max(seq // tm, 1)
    nb = mod3.shape[0]
    mod_map = (lambda i, j: (i // nt_seq, 0, 0)) if nb > 1 else (lambda i, j: (0, 0, 0))
    return pl.pallas_call(
        functools.partial(_inproj_kernel, rope=rope),
        out_shape=jax.ShapeDtypeStruct((m, ncols), BF16),
        grid=(m // tm, ncols // tn),
        in_specs=[pl.BlockSpec((tm, D_MODEL), lambda i, j: (i, 0)),
                  pl.BlockSpec((None, 3, D_MODEL), mod_map),
                  pl.BlockSpec((D_MODEL, tn), lambda i, j: (0, j)),
                  pl.BlockSpec((tm, 128), lambda i, j: (i % nt_seq, 0)),
                  pl.BlockSpec((tm, 128), lambda i, j: (i % nt_seq, 0))],
        out_specs=pl.BlockSpec((tm, tn), lambda i, j: (i, j)),
        scratch_shapes=[pltpu.VMEM((tm, D_MODEL), BF16)],
        compiler_params=_cparams(("parallel", "arbitrary")),
        name="inproj_rope" if rope else "inproj_ctx",
    )(x2d, mod3, w, cos_t, sin_t)


def _head_output(o1, o2, lamp_ref, g_ref, lam_init):
    lp = lamp_ref[...]
    lam = (jnp.exp(jnp.sum(lp[0:1] * lp[1:2], axis=1, keepdims=True))
           - jnp.exp(jnp.sum(lp[2:3] * lp[3:4], axis=1, keepdims=True)) + lam_init)
    od = o1 - lam * o2
    return od * lax.rsqrt(jnp.mean(od * od, axis=1, keepdims=True) + 1e-5) * g_ref[...] * (1.0 - lam_init)


def _attn_ctx_kernel(q_ref, kc_ref, vc_ref, lamp_ref, g_ref, o_ref, *, tq, lam_init):
    q = q_ref[...].astype(F32) * (HEAD_DIM ** -0.5)
    lane = lax.broadcasted_iota(jnp.int32, q.shape, 1)
    qz = jnp.concatenate([jnp.where(lane < HEAD_DIM, q, 0.0), jnp.where(lane >= HEAD_DIM, q, 0.0)],
                         axis=0).astype(BF16)
    s = lax.dot_general(qz, kc_ref[...], (((1,), (1,)), ((), ())), preferred_element_type=F32)
    p = jnp.exp(s - jnp.max(s, axis=1, keepdims=True))
    l = jnp.sum(p, axis=1, keepdims=True)
    o = jnp.dot(p.astype(BF16), vc_ref[...], preferred_element_type=F32) / l
    o_ref[...] = _head_output(o[0:tq], o[tq:], lamp_ref, g_ref, lam_init).astype(o_ref.dtype)


def _attention_ctx(proj_c, lamp, g, *, bsz, seq_c, lam_init):
    hq = U_Q * UNIT // 128
    hk = U_K * UNIT // 128
    hv = U_V * UNIT // 128
    return pl.pallas_call(
        functools.partial(_attn_ctx_kernel, tq=seq_c, lam_init=lam_init),
        out_shape=jax.ShapeDtypeStruct((bsz * seq_c, N_HEADS * V_DIM), BF16),
        grid=(bsz, N_HEADS),
        in_specs=[pl.BlockSpec((seq_c, 128), lambda b, h: (b, hq + h)),
                  pl.BlockSpec((seq_c, 128), lambda b, h: (b, hk + h)),
                  pl.BlockSpec((seq_c, 128), lambda b, h: (b, hv + h)),
                  pl.BlockSpec((4, HEAD_DIM), lambda b, h: (0, 0)),
                  pl.BlockSpec((1, V_DIM), lambda b, h: (0, 0))],
        out_specs=pl.BlockSpec((seq_c, V_DIM), lambda b, h: (b, h)),
        compiler_params=_cparams(("parallel", "parallel")),
        name="diff_attn_ctx",
    )(proj_c, proj_c, proj_c, lamp, g)


def _attn_lat_kernel(q_ref, k_ref, v_ref, kc_ref, vc_ref, lamp_ref, g_ref, o_ref,
                     qz, vext, vcext, sbuf, pbuf, abuf, m_s, acc, *, tq, nsub, ck, n_chunks, lam_init):
    @pl.when(pl.program_id(2) == 0)
    def _():
        vext[:, 0:V_DIM] = v_ref[...]
        vext[:, V_DIM:] = jnp.ones((vext.shape[0], V_DIM), BF16)
        vcext[:, 0:V_DIM] = vc_ref[...]
        vcext[:, V_DIM:] = jnp.ones((vcext.shape[0], V_DIM), BF16)

    subs = range(nsub)
    for h in subs:
        q = q_ref[h * tq:(h + 1) * tq, :].astype(F32) * (HEAD_DIM ** -0.5 * math.log2(math.e))
        lane = lax.broadcasted_iota(jnp.int32, q.shape, 1)
        qz[h, 0:tq, :] = jnp.where(lane < HEAD_DIM, q, 0.0).astype(BF16)
        qz[h, tq:, :] = jnp.where(lane >= HEAD_DIM, q, 0.0).astype(BF16)

    def qk(h, kblk):
        return lax.dot_general(qz[h], kblk, (((1,), (1,)), ((), ())), preferred_element_type=F32)

    def kchunk(c):
        return k_ref[pl.ds(pl.multiple_of(c * ck, ck), ck), :]

    def vchunk(c):
        return vext[pl.ds(pl.multiple_of(c * ck, ck), ck), :]

    def score(slot, c):
        kblk = kchunk(c)
        for h in subs:
            sbuf[h, slot] = qk(h, kblk)

    def softmax(slot):
        for h in subs:
            s = sbuf[h, slot]
            m_prev = m_s[h]
            m_new = jnp.maximum(m_prev, jnp.max(s, axis=1, keepdims=True))
            abuf[h, slot] = jnp.exp2(m_prev - m_new)
            pbuf[h, slot] = jnp.exp2(s - m_new).astype(BF16)
            m_s[h] = m_new

    def pv(slot, c):
        vblk = vchunk(c)
        for h in subs:
            acc[h] = abuf[h, slot] * acc[h] + jnp.dot(pbuf[h, slot], vblk, preferred_element_type=F32)

    for h in subs:
        s = qk(h, kc_ref[...])
        m0 = jnp.max(s, axis=1, keepdims=True)
        m_s[h] = m0
        acc[h] = jnp.dot(jnp.exp2(s - m0).astype(BF16), vcext[...], preferred_element_type=F32)

    score(0, 0)
    softmax(0)
    score(1, 1)

    def body(j, _):
        a = 2 * j
        pv(0, a)
        score(0, a + 2)
        softmax(1)
        pv(1, a + 1)
        score(1, a + 3)
        softmax(0)
        return 0
    lax.fori_loop(0, n_chunks // 2 - 1, body, 0)

    pv(0, n_chunks - 2)
    softmax(1)
    pv(1, n_chunks - 1)

    for h in subs:
        a_ = acc[h]
        o = a_[:, 0:V_DIM] / a_[:, V_DIM:]
        o_ref[h * tq:(h + 1) * tq, :] = _head_output(o[0:tq], o[tq:], lamp_ref, g_ref,
                                                     lam_init).astype(o_ref.dtype)


def _attention_lat(proj, proj_c, lamp, g, *, bsz, seq, seq_c, lam_init, tq, nsub, ck):
    tqs = tq * nsub
    nq = seq // tqs
    n_chunks = seq // ck
    assert n_chunks % 2 == 0 and n_chunks >= 2
    hq = U_Q * UNIT // 128
    hk = U_K * UNIT // 128
    hv = U_V * UNIT // 128
    return pl.pallas_call(
        functools.partial(_attn_lat_kernel, tq=tq, nsub=nsub, ck=ck, n_chunks=n_chunks, lam_init=lam_init),
        out_shape=jax.ShapeDtypeStruct((bsz * seq, N_HEADS * V_DIM), BF16),
        grid=(bsz, N_HEADS, nq),
        in_specs=[pl.BlockSpec((tqs, 128), lambda b, h, qi: (b * nq + qi, hq + h)),
                  pl.BlockSpec((seq, 128), lambda b, h, qi: (b, hk + h)),
                  pl.BlockSpec((seq, 128), lambda b, h, qi: (b, hv + h)),
                  pl.BlockSpec((seq_c, 128), lambda b, h, qi: (b, hk + h)),
                  pl.BlockSpec((seq_c, 128), lambda b, h, qi: (b, hv + h)),
                  pl.BlockSpec((4, HEAD_DIM), lambda b, h, qi: (0, 0)),
                  pl.BlockSpec((1, V_DIM), lambda b, h, qi: (0, 0))],
        out_specs=pl.BlockSpec((tqs, V_DIM), lambda b, h, qi: (b * nq + qi, h)),
        scratch_shapes=[pltpu.VMEM((nsub, 2 * tq, 128), BF16),
                        pltpu.VMEM((seq, 2 * V_DIM), BF16), pltpu.VMEM((seq_c, 2 * V_DIM), BF16),
                        pltpu.VMEM((nsub, 2, 2 * tq, ck), F32), pltpu.VMEM((nsub, 2, 2 * tq, ck), BF16),
                        pltpu.VMEM((nsub, 2, 2 * tq, 1), F32), pltpu.VMEM((nsub, 2 * tq, 1), F32),
                        pltpu.VMEM((nsub, 2 * tq, 2 * V_DIM), F32)],
        compiler_params=_cparams(("parallel", "parallel", "arbitrary")),
        name="diff_attn",
    )(proj, proj, proj, proj_c, proj_c, lamp, g)


def _conv3(cur, prv, nxt, w):
    tm = cur.shape[0]
    row = lax.broadcasted_iota(jnp.int32, cur.shape, 0)
    dn = jnp.where(row == 0, prv, pltpu.roll(cur, 1, 0))
    up = jnp.where(row == tm - 1, nxt, pltpu.roll(cur, tm - 1, 0))
    return dn * w[0:1] + cur * w[1:2] + up * w[2:3]


def _convgate_kernel(main_ref, prev_ref, next_ref, ab_ref, ca_ref, ch_ref,
                     ya_ref, v0_ref, x1_ref, x2_ref, *, nt):
    i = pl.program_id(1)
    w_ = BRANCH_W
    m = main_ref[...].astype(F32)
    pv = jnp.where(i > 0, prev_ref[15:16, :].astype(F32), 0.0)
    nx = jnp.where(i < nt - 1, next_ref[0:1, :].astype(F32), 0.0)
    ab = ab_ref[...].astype(F32)
    p = m[:, 0:w_] * m[:, w_:2 * w_]
    p_prev = pv[:, 0:w_] * pv[:, w_:2 * w_]
    p_next = nx[:, 0:w_] * nx[:, w_:2 * w_]
    ya = ab[:, 0:w_] * _conv3(p, p_prev, p_next, ca_ref[...]) * _silu(ab[:, w_:])
    ya_ref[...] = ya.astype(ya_ref.dtype)
    ch = ch_ref[...]
    for n, ref in enumerate((v0_ref, x1_ref, x2_ref)):
        lo, hi = (2 + n) * w_, (3 + n) * w_
        ref[...] = _conv3(m[:, lo:hi], pv[:, lo:hi], nx[:, lo:hi], ch[:, n * w_:(n + 1) * w_]).astype(ref.dtype)


def _convgate(proj, conv_a, conv_h, *, bsz, seq, tm):
    m = bsz * seq
    nt = seq // tm
    cw = 5 * UNIT
    hb = 16
    nhb = m // hb
    out = jax.ShapeDtypeStruct((m, BRANCH_W), BF16)
    ospec = pl.BlockSpec((tm, BRANCH_W), lambda b, i: (b * nt + i, 0))
    return pl.pallas_call(
        functools.partial(_convgate_kernel, nt=nt),
        out_shape=(out, out, out, out),
        grid=(bsz, nt),
        in_specs=[pl.BlockSpec((tm, cw), lambda b, i: (b * nt + i, U_CONV * UNIT // cw)),
                  pl.BlockSpec((hb, cw), lambda b, i: (jnp.maximum((b * nt + i) * (tm // hb) - 1, 0), 1)),
                  pl.BlockSpec((hb, cw), lambda b, i: (jnp.minimum((b * nt + i + 1) * (tm // hb), nhb - 1), 1)),
                  pl.BlockSpec((tm, 2 * UNIT), lambda b, i: (b * nt + i, U_AB * UNIT // (2 * UNIT))),
                  pl.BlockSpec((3, BRANCH_W), lambda b, i: (0, 0)),
                  pl.BlockSpec((3, 3 * BRANCH_W), lambda b, i: (0, 0))],
        out_specs=(ospec, ospec, ospec, ospec),
        compiler_params=_cparams(("parallel", "parallel")),
        name="convgate",
    )(proj, proj, proj, proj, conv_a, conv_h)


def _taps_kernel(cols_ref, w1c_ref, w1s_ref, w2_ref, w3_ref, dl_ref, taps_ref, asum_ref, *, seq, tr):
    i = pl.program_id(0)

    def source(m):
        return jnp.where(m < seq, m, 2 * seq - m).astype(F32)

    src_l = source(i * tr + lax.broadcasted_iota(jnp.int32, (1, tr), 1))
    t_l = src_l / (seq - 1.0)
    w_l = (2.0 * math.pi / seq) * src_l
    band = lax.broadcasted_iota(jnp.int32, (HYENA_BANDS, 1), 0).astype(F32)
    f = 1e-4 + band * ((HYENA_BANDS - 1.0 - 1e-4) / (HYENA_BANDS - 1.0))
    ang = f * w_l
    cols = cols_ref[...]
    pre = (cols[:, 0:1] * t_l
           + jnp.dot(w1c_ref[...], jnp.cos(ang), preferred_element_type=F32, precision=HIGHEST)
           + jnp.dot(w1s_ref[...], -jnp.sin(ang), preferred_element_type=F32, precision=HIGHEST)
           + cols[:, 1:2])
    h = jnp.sin(cols[:, 3:4] * pre)
    h = jnp.sin(cols[:, 4:5] * (jnp.dot(w2_ref[...], h, preferred_element_type=F32, precision=HIGHEST)
                                + cols[:, 2:3]))
    w3 = w3_ref[...]
    h_hi, w_hi = h.astype(BF16), w3.astype(BF16)
    h_lo, w_lo = (h - h_hi.astype(F32)).astype(BF16), (w3 - w_hi.astype(F32)).astype(BF16)
    tn_dot = lambda a, b: lax.dot_general(a, b, (((0,), (0,)), ((), ())), preferred_element_type=F32)
    h = tn_dot(h_hi, w_hi) + (tn_dot(h_lo, w_hi) + tn_dot(h_hi, w_lo))
    mrow = i * tr + lax.broadcasted_iota(jnp.int32, (tr, 1), 0)
    t = source(mrow) / (seq - 1.0)
    decay = jnp.exp(-t * jnp.abs(dl_ref[...]))
    out = h * (decay + HYENA_SHIFT)
    out = jnp.where(mrow == seq, 0.0, out)
    taps_ref[...] = out

    @pl.when(i == 0)
    def _():
        asum_ref[...] = jnp.zeros(asum_ref.shape, F32)

    asum_ref[...] += jnp.sum(jnp.abs(out), axis=0, keepdims=True)


def _hyena_taps(hp, *, seq, tr):
    w1, b1, w2, b2, w3, freq, deltas = hp
    hh = HYENA_HIDDEN
    cw = 2 * BRANCH_W
    w3d = w3.reshape(hh, 2, 2, BRANCH_W).transpose(2, 0, 1, 3).reshape(2, hh, cw)
    dld = deltas.transpose(1, 0, 2).reshape(2, 1, cw)
    cols = jnp.stack([w1[0], b1, b2, freq[0], freq[1]], axis=1)
    nt = 2 * seq // tr
    full = lambda shape: pl.BlockSpec(shape, lambda i: tuple(0 for _ in shape))
    return pl.pallas_call(
        functools.partial(_taps_kernel, seq=seq, tr=tr),
        out_shape=(jax.ShapeDtypeStruct((2 * seq, cw), F32), jax.ShapeDtypeStruct((1, cw), F32)),
        grid=(nt,),
        in_specs=[full((hh, 5)), full((hh, HYENA_BANDS)), full((hh, HYENA_BANDS)), full((hh, hh)),
                  pl.BlockSpec((None, hh, cw), lambda i: (i // (nt // 2), 0, 0)),
                  pl.BlockSpec((None, 1, cw), lambda i: (i // (nt // 2), 0, 0))],
        out_specs=(pl.BlockSpec((tr, cw), lambda i: (i, 0)), pl.BlockSpec((1, cw), lambda i: (0, 0))),
        compiler_params=_cparams(("arbitrary",)),
        name="hyena_taps",
    )(cols, w1[1:1 + HYENA_BANDS].T, w1[1 + HYENA_BANDS:].T, w2.T, w3d, dld)


def _lmm_kernel(a_ref, x_ref, *rest, gate):
    if gate:
        xg_ref, z_ref, bias_ref, o_ref = rest
    else:
        (o_ref,) = rest
    acc = jnp.dot(a_ref[...], x_ref[...].astype(BF16), preferred_element_type=F32)
    if gate:
        acc = xg_ref[...].astype(F32) * (acc + bias_ref[...] * z_ref[...].astype(F32))
    o_ref[...] = acc.astype(o_ref.dtype)


def _lmm(a, x, *, tn, out_dtype, gate_args=None, name):
    nb, k, n = x.shape
    ma = a.shape[0]
    in_specs = [pl.BlockSpec((ma, k), lambda b, j: (0, 0)), pl.BlockSpec((None, k, tn), lambda b, j: (b, 0, j))]
    args = [a, x]
    if gate_args is not None:
        xg, z, bias_t = gate_args
        in_specs += [pl.BlockSpec((None, ma, tn), lambda b, j: (b, 0, j)),
                     pl.BlockSpec((None, ma, tn), lambda b, j: (b, 0, j)),
                     pl.BlockSpec((1, tn), lambda b, j: (0, 0))]
        args += [xg, z, bias_t]
    return pl.pallas_call(
        functools.partial(_lmm_kernel, gate=gate_args is not None),
        out_shape=jax.ShapeDtypeStruct((nb, ma, n), out_dtype),
        grid=(nb, n // tn),
        in_specs=in_specs,
        out_specs=pl.BlockSpec((None, ma, tn), lambda b, j: (b, 0, j)),
        compiler_params=_cparams(("parallel", "parallel")),
        name=name,
    )(*args)


def _mid_kernel(a_ref, t_ref, *rest, conv):
    if conv:
        g_ref, o_ref, xbuf, ybuf = rest
    else:
        asum_ref, o_ref = rest
    kb, n2 = a_ref.shape[1], a_ref.shape[2]

    def forward(kk):
        a = jnp.concatenate([a_ref[0, kk], a_ref[1, kk]], axis=0)
        return jnp.dot(t_ref[kk], a, preferred_element_type=F32)

    if not conv:
        for kk in range(kb):
            xk = forward(kk) / (asum_ref[...] + 1e-6)
            o_ref[kk, 0] = xk[0:n2].astype(o_ref.dtype)
            o_ref[kk, 1] = xk[n2:].astype(o_ref.dtype)
        return

    for kk in range(kb):
        xbuf[kk] = forward(kk)
    for kk in range(kb):
        xr, xi = xbuf[kk, 0:n2], xbuf[kk, n2:]
        gr, gi = g_ref[kk, 0].astype(F32), g_ref[kk, 1].astype(F32)
        ybuf[kk, 0:n2] = (xr * gr - xi * gi).astype(BF16)
        ybuf[kk, n2:] = (xr * gi + xi * gr).astype(BF16)
    for kk in range(kb):
        bk = lax.dot_general(t_ref[kk], ybuf[kk], (((0,), (0,)), ((), ())), preferred_element_type=F32)
        o_ref[0, kk] = bk[0:n2].astype(o_ref.dtype)
        o_ref[1, kk] = bk[n2:].astype(o_ref.dtype)


def _mid_spectrum(a5, tmat, asum):
    _, _, n1, n2, cw = a5.shape
    kb = _pick(n1, (MID_K1_PER_STEP,))
    return pl.pallas_call(
        functools.partial(_mid_kernel, conv=False),
        out_shape=jax.ShapeDtypeStruct((n1, 2, n2, cw), BF16),
        grid=(n1 // kb,),
        in_specs=[pl.BlockSpec((None, 2, kb, n2, cw), lambda k: (0, 0, k, 0, 0)),
                  pl.BlockSpec((kb, 2 * n2, 2 * n2), lambda k: (k, 0, 0)),
                  pl.BlockSpec((1, cw), lambda k: (0, 0))],
        out_specs=pl.BlockSpec((kb, 2, n2, cw), lambda k: (k, 0, 0, 0)),
        compiler_params=_cparams(("parallel",)),
        name="hyena_spectrum",
    )(a5, tmat, asum)


def _mid_conv(a5, tmat, gspec, order):
    nb, _, n1, n2, cw = a5.shape
    kb = _pick(n1, (MID_K1_PER_STEP,))
    return pl.pallas_call(
        functools.partial(_mid_kernel, conv=True),
        out_shape=jax.ShapeDtypeStruct((nb, 2, n1, n2, cw), BF16),
        grid=(n1 // kb, nb),
        in_specs=[pl.BlockSpec((None, 2, kb, n2, cw), lambda k, b: (b, 0, k, 0, 0)),
                  pl.BlockSpec((kb, 2 * n2, 2 * n2), lambda k, b: (k, 0, 0)),
                  pl.BlockSpec((kb, 2, n2, cw), lambda k, b: (k, 0, 0, order))],
        out_specs=pl.BlockSpec((None, 2, kb, n2, cw), lambda k, b: (b, 0, k, 0, 0)),
        scratch_shapes=[pltpu.VMEM((kb, 2 * n2, cw), F32), pltpu.VMEM((kb, 2 * n2, cw), BF16)],
        compiler_params=_cparams(("parallel", "arbitrary")),
        name="hyena_mid",
    )(a5, tmat, gspec)


def _dft_tables(seq):
    n = 2 * seq
    n2 = DFT_N2
    n1 = n // n2
    two_pi = 2.0 * math.pi

    def cs(num, den):
        ang = (num % den).astype(F32) * (two_pi / den)
        return jnp.cos(ang), jnp.sin(ang)

    k1 = jnp.arange(n1, dtype=jnp.int32)
    c, s = cs(k1[:, None] * jnp.arange(n1, dtype=jnp.int32)[None, :], n1)
    f1_full = jnp.concatenate([c, -s], axis=0)
    i2 = jnp.arange(n2, dtype=jnp.int32)
    num = i2[None, None, :] * k1[:, None, None] + n1 * (i2[None, :, None] * i2[None, None, :])
    c, s = cs(num, n)
    tre, tim = c, -s
    tmat = jnp.concatenate([jnp.concatenate([tre, -tim], axis=2),
                            jnp.concatenate([tim, tre], axis=2)], axis=1)
    c, s = cs(jnp.arange(n1 // 2, dtype=jnp.int32)[:, None] * k1[None, :], n1)
    fi = jnp.concatenate([c, -s], axis=1) * (1.0 / n)
    return f1_full.astype(BF16), f1_full[:, :n1 // 2].astype(BF16), tmat.astype(BF16), fi.astype(BF16)


def _dft_tables_small(seq):
    n = 2 * seq
    k = jnp.arange(n, dtype=jnp.int32)
    ang = ((k[:, None] * k[None, :]) % n).astype(F32) * (2.0 * math.pi / n)
    c, s = jnp.cos(ang), jnp.sin(ang)
    fwd = jnp.concatenate([c, -s], axis=0)
    inv = jnp.concatenate([c[:seq], -s[:seq]], axis=1) * (1.0 / n)
    return fwd.astype(BF16), fwd[:, :seq].astype(BF16), inv.astype(BF16)


def _ctxconv_kernel(z_ref, xg_ref, taps_ref, asum_ref, bias_ref, ff_ref, fh_ref, fi_ref, o_ref):
    n = ff_ref.shape[1]
    z = z_ref[...]
    g = jnp.dot(ff_ref[...], taps_ref[...].astype(BF16), preferred_element_type=F32) / (asum_ref[...] + 1e-6)
    xk = jnp.dot(fh_ref[...], z, preferred_element_type=F32)
    xr, xi = xk[0:n], xk[n:]
    gr, gi = g[0:n], g[n:]
    yk = jnp.concatenate([xr * gr - xi * gi, xr * gi + xi * gr], axis=0).astype(BF16)
    y = jnp.dot(fi_ref[...], yk, preferred_element_type=F32)
    o_ref[...] = (xg_ref[...].astype(F32) * (y + bias_ref[...] * z.astype(F32))).astype(o_ref.dtype)


def _ctx_longconv(z, xg, taps, asum, bias, tabs, order, *, bsz, seq):
    ff, fh, fi = tabs
    n = 2 * seq
    cw = BRANCH_W
    return pl.pallas_call(
        _ctxconv_kernel,
        out_shape=jax.ShapeDtypeStruct((bsz * seq, cw), BF16),
        grid=(bsz,),
        in_specs=[pl.BlockSpec((seq, cw), lambda b: (b, 0)),
                  pl.BlockSpec((seq, cw), lambda b: (b, 0)),
                  pl.BlockSpec((n, cw), lambda b: (0, order)),
                  pl.BlockSpec((1, cw), lambda b: (0, order)),
                  pl.BlockSpec((1, cw), lambda b: (0, 0)),
                  pl.BlockSpec((2 * n, n), lambda b: (0, 0)),
                  pl.BlockSpec((2 * n, seq), lambda b: (0, 0)),
                  pl.BlockSpec((seq, 2 * n), lambda b: (0, 0))],
        out_specs=pl.BlockSpec((seq, cw), lambda b: (b, 0)),
        compiler_params=_cparams(("parallel",)),
        name="hyena_ctx_conv",
    )(z, xg, taps, asum, bias, ff, fh, fi)


def _hyena_long(v0, x1, x2, taps, asum, hy_bias, tabs, *, bsz, seq):
    f1_full, f1_half, tmat, fi = tabs
    n2 = DFT_N2
    n1 = 2 * seq // n2
    cw = BRANCH_W
    tn = 8192
    at = _lmm(f1_full, taps.reshape(1, n1, n2 * 2 * cw), tn=tn, out_dtype=BF16, name="taps_dft1")
    gspec = _mid_spectrum(at.reshape(1, 2, n1, n2, 2 * cw), tmat, asum)

    def conv(z, xg, order):
        z2 = z.reshape(bsz, n1 // 2, n2 * cw)
        a = _lmm(f1_half, z2, tn=tn, out_dtype=BF16, name="hyena_dft1")
        bk = _mid_conv(a.reshape(bsz, 2, n1, n2, cw), tmat, gspec, order)
        bias_t = jnp.tile(hy_bias[order].reshape(1, cw), (1, tn // cw))
        y = _lmm(fi, bk.reshape(bsz, 2 * n1, n2 * cw), tn=tn, out_dtype=BF16,
                 gate_args=(xg.reshape(bsz, n1 // 2, n2 * cw), z2, bias_t), name="hyena_idft2")
        return y.reshape(bsz * seq, cw)

    return conv(conv(v0, x1, 0), x2, 1)


def _s5_kernel(u_ref, bre_ref, bim_ref, cre_ref, cim_ref, are_ref, aim_ref, s0_ref,
               y_ref, sfin_ref,
               uf, lhs, bur, bui, ybuf, pw_r, pw_i, car_r, car_i, st_r, st_i, *, tseg, rev):
    i = pl.program_id(1)
    sw = S5_SW
    gw = sw // S5_SUPER
    a_re = are_ref[...]
    a_im = aim_ref[...]

    @pl.when(jnp.logical_and(pl.program_id(0) == 0, i == 0))
    def _():
        def pbody(r, carry):
            cr, ci = carry
            pw_r[pl.ds(r, 1), :] = cr
            pw_i[pl.ds(r, 1), :] = ci
            return cr * a_re - ci * a_im, cr * a_im + ci * a_re
        lax.fori_loop(0, tseg, pbody, (a_re, a_im))

    @pl.when(i == 0)
    def _():
        st_r[...] = s0_ref[:, 0:sw]
        st_i[...] = s0_ref[:, sw:]

    for sg in range(S5_SUPER):
        uf[sg] = u_ref[:, sg * 128:(sg + 1) * 128].astype(F32)

    def gather(r, _):
        for sg in range(S5_SUPER):
            lhs[sg, pl.ds(pl.multiple_of(r * 8, 8), 8), :] = uf[sg, pl.ds(r, 8, stride=tseg), :]
        return 0
    lax.fori_loop(0, tseg, gather, 0, unroll=S5_UNROLL)

    for sg in range(S5_SUPER):
        lb = lhs[sg].astype(BF16)
        bur[:, sg * gw:(sg + 1) * gw] = jnp.dot(lb, bre_ref[sg], preferred_element_type=F32)
        bui[:, sg * gw:(sg + 1) * gw] = jnp.dot(lb, bim_ref[sg], preferred_element_type=F32)

    for sg in range(S5_SUPER):
        cols = slice(sg * gw, (sg + 1) * gw)
        ar = jnp.broadcast_to(a_re[:, cols], (8, gw))
        ai = jnp.broadcast_to(a_im[:, cols], (8, gw))

        def sbody(k, carry, cols=cols, ar=ar, ai=ai):
            sr, si = carry
            r = (tseg - 1 - k) if rev else k
            rows = pl.ds(pl.multiple_of(r * 8, 8), 8)
            nr = ar * sr - ai * si + bur[rows, cols]
            ni = ar * si + ai * sr + bui[rows, cols]
            bur[rows, cols] = nr
            bui[rows, cols] = ni
            return nr, ni
        zero = jnp.zeros((8, gw), F32)
        lax.fori_loop(0, tseg, sbody, (zero, zero))

    at_r = pw_r[tseg - 1:tseg, :]
    at_i = pw_i[tseg - 1:tseg, :]
    end_row = 0 if rev else (tseg - 1) * 8
    cr = st_r[...]
    ci = st_i[...]
    order = range(7, -1, -1) if rev else range(8)
    for s in order:
        car_r[s:s + 1, :] = cr
        car_i[s:s + 1, :] = ci
        er = bur[end_row + s:end_row + s + 1, :]
        ei = bui[end_row + s:end_row + s + 1, :]
        cr, ci = er + at_r * cr - at_i * ci, ei + at_r * ci + at_i * cr
    st_r[...] = cr
    st_i[...] = ci
    sfin_ref[:, 0:sw] = cr
    sfin_ref[:, sw:] = ci

    for sg in range(S5_SUPER):
        cols = slice(sg * gw, (sg + 1) * gw)
        kr = car_r[:, cols]
        kim = car_i[:, cols]

        def fbody(r, _, cols=cols, kr=kr, kim=kim):
            pidx = (tseg - 1 - r) if rev else r
            pr = pw_r[pl.ds(pidx, 1), cols]
            pi_ = pw_i[pl.ds(pidx, 1), cols]
            rows = pl.ds(pl.multiple_of(r * 8, 8), 8)
            bur[rows, cols] = bur[rows, cols] + (pr * kr - pi_ * kim)
            bui[rows, cols] = bui[rows, cols] + (pr * kim + pi_ * kr)
            return 0
        lax.fori_loop(0, tseg, fbody, 0, unroll=S5_UNROLL)

    for sg in range(S5_SUPER):
        cols = slice(sg * gw, (sg + 1) * gw)
        ybuf[sg] = (jnp.dot(bur[:, cols].astype(BF16), cre_ref[sg], preferred_element_type=F32)
                    + jnp.dot(bui[:, cols].astype(BF16), cim_ref[sg], preferred_element_type=F32))

    def scatter(r, _):
        for sg in range(S5_SUPER):
            y_ref[sg, pl.ds(r, 8, stride=tseg), :] = ybuf[sg, pl.ds(pl.multiple_of(r * 8, 8), 8), :]
        return 0
    lax.fori_loop(0, tseg, scatter, 0, unroll=S5_UNROLL)


def _s5_scan(proj, s5p, s0, *, bsz, seq, tseg, rev):
    bre, bim, cre, cim, are, aim = s5p
    tr = 8 * tseg
    nt = seq // tr
    sw = S5_SW
    gw = sw // S5_SUPER
    tile = (lambda b, i: (b * nt + (nt - 1 - i), U_U)) if rev else (lambda b, i: (b * nt + i, U_U))
    otile = (lambda b, i: (0, b * nt + (nt - 1 - i), 0)) if rev else (lambda b, i: (0, b * nt + i, 0))
    full = lambda shape: pl.BlockSpec(shape, lambda b, i: tuple(0 for _ in shape))
    return pl.pallas_call(
        functools.partial(_s5_kernel, tseg=tseg, rev=rev),
        out_shape=(jax.ShapeDtypeStruct((S5_SUPER, bsz * seq, 128), F32),
                   jax.ShapeDtypeStruct((bsz, 1, 2 * sw), F32)),
        grid=(bsz, nt),
        in_specs=[pl.BlockSpec((tr, UNIT), tile),
                  full((S5_SUPER, 128, gw)), full((S5_SUPER, 128, gw)),
                  full((S5_SUPER, gw, 128)), full((S5_SUPER, gw, 128)),
                  full((1, sw)), full((1, sw)),
                  pl.BlockSpec((None, 1, 2 * sw), lambda b, i: (b, 0, 0))],
        out_specs=(pl.BlockSpec((S5_SUPER, tr, 128), otile),
                   pl.BlockSpec((None, 1, 2 * sw), lambda b, i: (b, 0, 0))),
        scratch_shapes=[pltpu.VMEM((S5_SUPER, tr, 128), F32), pltpu.VMEM((S5_SUPER, tr, 128), F32),
                        pltpu.VMEM((tr, sw), F32), pltpu.VMEM((tr, sw), F32),
                        pltpu.VMEM((S5_SUPER, tr, 128), F32),
                        pltpu.VMEM((tseg, sw), F32), pltpu.VMEM((tseg, sw), F32),
                        pltpu.VMEM((8, sw), F32), pltpu.VMEM((8, sw), F32),
                        pltpu.VMEM((1, sw), F32), pltpu.VMEM((1, sw), F32)],
        compiler_params=_cparams(("arbitrary", "arbitrary")),
        name="s5_bwd" if rev else "s5_fwd",
    )(proj, bre, bim, cre, cim, are, aim, s0)


def _s5_params(a_re, a_im, log_step, b_re, b_im, c_re, c_im):
    g, p, ci = S5_GROUPS, S5_STATE, S5_GROUP
    dt = jnp.exp(log_step)[:, None]
    mag = jnp.exp(a_re * dt)
    ar, ai = mag * jnp.cos(a_im * dt), mag * jnp.sin(a_im * dt)
    den = a_re * a_re + a_im * a_im
    fr = ((ar - 1.0) * a_re + ai * a_im) / den
    fi = (ai * a_re - (ar - 1.0) * a_im) / den
    bbr = fr[..., None] * b_re - fi[..., None] * b_im
    bbi = fr[..., None] * b_im + fi[..., None] * b_re
    eye = jnp.eye(8, dtype=F32)

    def blockdiag_b(m):
        m4 = m.reshape(S5_SUPER, 8, p, ci)
        return jnp.einsum('sgpc,gh->sgchp', m4, eye).reshape(S5_SUPER, 8 * ci, 8 * p).astype(BF16)

    def blockdiag_c(m):
        m4 = m.reshape(S5_SUPER, 8, ci, p)
        return jnp.einsum('sgcp,gh->sgphc', m4, eye).reshape(S5_SUPER, 8 * p, 8 * ci).astype(BF16)

    return (blockdiag_b(bbr), blockdiag_b(bbi), blockdiag_c(c_re), blockdiag_c(-c_im),
            ar.reshape(1, g * p), ai.reshape(1, g * p))


def _merge_kernel(ya_ref, yh_ref, hg_ref, yatt_ref, attg_ref, ysf_ref, ysb_ref, u_ref, s5g_ref,
                  mg0_ref, mg1_ref, mg2_ref, mg3_ref, x_ref, mod_ref, s5d_ref, wglu_ref, wb_ref, wout_ref,
                  lng_ref, lnb_ref, o_ref, *, alpha):
    f = lambda r: r[...].astype(F32)
    y_a = f(ya_ref)
    y_h = f(yh_ref) * _silu(f(hg_ref))
    y_c = f(yatt_ref) * _silu(f(attg_ref))
    ys = jnp.concatenate([ysf_ref[sg] + ysb_ref[sg] for sg in range(S5_SUPER)], axis=1)
    y = ys + s5d_ref[...] * f(u_ref)
    zg = jax.nn.gelu(y)
    glu = jnp.dot(zg.astype(BF16), wglu_ref[...], preferred_element_type=F32)
    y_d = zg * _sigmoid(glu) * _silu(f(s5g_ref))
    mix = None
    for n, (yn, mg) in enumerate(((y_a, mg0_ref), (y_h, mg1_ref), (y_c, mg2_ref), (y_d, mg3_ref))):
        term = _sigmoid(f(mg)) * jnp.dot(yn.astype(BF16), wb_ref[n], preferred_element_type=F32)
        mix = term if mix is None else mix + term
    out = jnp.dot(mix.astype(BF16), wout_ref[...], preferred_element_type=F32)
    gate = mod_ref[2:3, :]
    r = alpha * x_ref[...] + gate * out
    mu = jnp.mean(r, axis=1, keepdims=True)
    rc = r - mu
    var = jnp.mean(rc * rc, axis=1, keepdims=True)
    o_ref[...] = rc * lax.rsqrt(var + LN_EPS) * lng_ref[...] + lnb_ref[...]


def _merge(ya, yh, yatt, ysf, ysb, proj, x2d, mod3, s5d, wglu, wb, wout, lng, lnb, *, seq, tm, alpha):
    m = x2d.shape[0]
    nt_seq = seq // tm
    nb = mod3.shape[0]
    mod_map = (lambda i: (i // nt_seq, 0, 0)) if nb > 1 else (lambda i: (0, 0, 0))
    row = lambda w_: pl.BlockSpec((tm, w_), lambda i: (i, 0))
    pcol = lambda unit: pl.BlockSpec((tm, UNIT), lambda i: (i, unit))
    mcol = lambda n: pl.BlockSpec((tm, 2 * UNIT), lambda i: (i, U_MERGE // 2 + n))
    full = lambda shape: pl.BlockSpec(shape, lambda i: tuple(0 for _ in shape))
    w_ = BRANCH_W
    s5row = pl.BlockSpec((S5_SUPER, tm, 128), lambda i: (0, i, 0))
    return pl.pallas_call(
        functools.partial(_merge_kernel, alpha=alpha),
        out_shape=jax.ShapeDtypeStruct((m, D_MODEL), F32),
        grid=(m // tm,),
        in_specs=[row(w_), row(w_), pcol(U_HG), row(w_), pcol(U_ATTG), s5row, s5row, pcol(U_U), pcol(U_S5G),
                  mcol(0), mcol(1), mcol(2), mcol(3), row(D_MODEL),
                  pl.BlockSpec((None, 3, D_MODEL), mod_map),
                  full((1, w_)), full((w_, w_)), full((N_BRANCH, w_, D_MODEL)), full((D_MODEL, D_MODEL)),
                  full((1, D_MODEL)), full((1, D_MODEL))],
        out_specs=row(D_MODEL),
        compiler_params=_cparams(("parallel",)),
        name="merge_out_norm",
    )(ya, yh, proj, yatt, proj, ysf, ysb, proj, proj, proj, proj, proj, proj, x2d, mod3,
      s5d, wglu, wb, wout, lng, lnb)


def _rope_tables(n_lat):
    rows = n_lat // GRID_W
    row = jnp.broadcast_to(jnp.arange(rows)[:, None], (rows, GRID_W)).reshape(-1)
    col = jnp.broadcast_to(jnp.arange(GRID_W)[None, :], (rows, GRID_W)).reshape(-1)
    half = HEAD_DIM // 2
    inv = 1.0 / (ROPE_BASE ** (jnp.arange(0, half, 2, dtype=F32) / half))
    ar, ac = row[:, None] * inv, col[:, None] * inv
    cos64 = jnp.concatenate([jnp.cos(ar), jnp.cos(ar), jnp.cos(ac), jnp.cos(ac)], axis=-1)
    sin64 = jnp.concatenate([-jnp.sin(ar), jnp.sin(ar), -jnp.sin(ac), jnp.sin(ac)], axis=-1)
    return jnp.tile(cos64, (1, 2)), jnp.tile(sin64, (1, 2))


def _pick(n, prefs):
    for t in prefs:
        if n % t == 0:
            return t
    return n


def kernel(x, c, ctx, c_ctx, w_mod, b_mod, w_in, conv_a, conv_h, hy_w1, hy_b1, hy_w2, hy_b2, hy_w3, hy_freq,
           hy_delta, hy_bias, lam_q1, lam_k1, lam_q2, lam_k2, attn_norm_g, s5_a_re, s5_a_im, s5_log_step,
           s5_b_re, s5_b_im, s5_c_re, s5_c_im, s5_d, s5_w_glu, w_branch, w_out, ln_g, ln_b):
    bsz, seq, d = x.shape
    seq_c = ctx.shape[1]
    depth = w_in.shape[0]
    alpha = (2.0 * depth) ** 0.25
    assert d == D_MODEL and seq % 1024 == 0 and seq_c % 256 == 0 and bsz + 1 <= 8

    cos_t, sin_t = _rope_tables(seq)
    dft_lat = _dft_tables(seq)
    dft_ctx = _dft_tables_small(seq_c)
    cvec = jnp.zeros((8, d), F32).at[0:bsz].set(c).at[bsz].set(c_ctx)
    tm_in = _pick(seq, (2048, 1024))
    tm_el = _pick(seq, (512,))
    tq = _pick(seq, (ATTN_TQ,))
    ck = _pick(seq, (1024, 512))
    tseg_lat = 128
    tseg_ctx = seq_c // 8

    x2 = x.reshape(bsz * seq, d)
    xc2 = ctx.reshape(bsz * seq_c, d)
    for l in range(depth):
        last = l == depth - 1
        lam_init = 0.8 - 0.6 * math.exp(-0.3 * l)
        mod = _modulation(cvec, w_mod, b_mod, l).reshape(8, 3, d)
        mod_lat, mod_ctx = mod[0:bsz], mod[bsz:bsz + 1]
        w_l = jnp.concatenate([w_in[l, :, u * UNIT:(u + 1) * UNIT] for u in UNIT_PERM], axis=1).astype(BF16)
        lamp = jnp.stack([lam_q1[l], lam_k1[l], lam_q2[l], lam_k2[l]], axis=0)
        g_att = attn_norm_g[l].reshape(1, V_DIM)
        s5f = _s5_params(s5_a_re[l, 0], s5_a_im[l, 0], s5_log_step[l, 0], s5_b_re[l, 0], s5_b_im[l, 0],
                         s5_c_re[l, 0], s5_c_im[l, 0])
        s5b = _s5_params(s5_a_re[l, 1], s5_a_im[l, 1], s5_log_step[l, 1], s5_b_re[l, 1], s5_b_im[l, 1],
                         s5_c_re[l, 1], s5_c_im[l, 1])
        hp = (hy_w1[l], hy_b1[l], hy_w2[l], hy_b2[l], hy_w3[l], hy_freq[l], hy_delta[l])
        wglu = s5_w_glu[l].astype(BF16)
        wb = w_branch[l].astype(BF16)
        wout = w_out[l].astype(BF16)
        s5d = s5_d[l].reshape(1, BRANCH_W)
        lng, lnb = ln_g[l].reshape(1, d), ln_b[l].reshape(1, d)

        ncols_c = 3 * 1024 if last else PROJ_W
        projc = _inproj(xc2, mod_ctx, w_l[:, :ncols_c], cos_t, sin_t, seq=seq_c, rope=False,
                        tm=bsz * seq_c, ncols=ncols_c)
        zero_state = jnp.zeros((bsz, 1, 2 * S5_SW), F32)
        ycf, scf = _s5_scan(projc, s5f, zero_state, bsz=bsz, seq=seq_c, tseg=tseg_ctx, rev=False)
        ycb, scb = _s5_scan(projc, s5b, zero_state, bsz=bsz, seq=seq_c, tseg=tseg_ctx, rev=True)

        proj = _inproj(x2, mod_lat, w_l, cos_t, sin_t, seq=seq, rope=True, tm=tm_in, ncols=PROJ_W)
        yatt = _attention_lat(proj, projc, lamp, g_att, bsz=bsz, seq=seq, seq_c=seq_c,
                              lam_init=lam_init, tq=tq, nsub=ATTN_SUBTILES, ck=ck)
        ysf, _ = _s5_scan(proj, s5f, scf, bsz=bsz, seq=seq, tseg=tseg_lat, rev=False)
        ysb, _ = _s5_scan(proj, s5b, scb, bsz=bsz, seq=seq, tseg=tseg_lat, rev=True)
        ya, v0, x1, x2h = _convgate(proj, conv_a[l], conv_h[l], bsz=bsz, seq=seq, tm=tm_el)
        taps, asum = _hyena_taps(hp, seq=seq, tr=1024)
        yh = _hyena_long(v0, x1, x2h, taps, asum, hy_bias[l], dft_lat, bsz=bsz, seq=seq)
        x_new = _merge(ya, yh, yatt, ysf, ysb, proj, x2, mod_lat, s5d, wglu, wb, wout, lng, lnb,
                       seq=seq, tm=tm_el, alpha=alpha)

        if not last:
            yatt_c = _attention_ctx(projc, lamp, g_att, bsz=bsz, seq_c=seq_c, lam_init=lam_init)
            ya_c, v0c, x1c, x2c = _convgate(projc, conv_a[l], conv_h[l], bsz=bsz, seq=seq_c, tm=seq_c)
            taps_c, asum_c = _hyena_taps(hp, seq=seq_c, tr=seq_c)
            b0 = hy_bias[l, 0].reshape(1, BRANCH_W)
            b1 = hy_bias[l, 1].reshape(1, BRANCH_W)
            z1c = _ctx_longconv(v0c, x1c, taps_c, asum_c, b0, dft_ctx, 0, bsz=bsz, seq=seq_c)
            yh_c = _ctx_longconv(z1c, x2c, taps_c, asum_c, b1, dft_ctx, 1, bsz=bsz, seq=seq_c)
            xc2 = _merge(ya_c, yh_c, yatt_c, ycf, ycb, projc, xc2, mod_ctx, s5d, wglu, wb, wout, lng, lnb,
                         seq=seq_c, tm=seq_c, alpha=alpha)
        x2 = x_new
    return x2.reshape(bsz, seq, d)
```
